```python
import jax, jax.numpy as jnp
from jax import lax
import numpy as np

D_MODEL = 1024
BATCH = 16
SEQ = 2048
DEPTH = 4
DEC_BATCH = 128
DEC_SEQ = 1
PAST_LEN = 8192
PAGE_SIZE = 128

RWKV_HEADS = 8
RWKV_HEAD_DIM = 64
RWKV_DIM = RWKV_HEADS * RWKV_HEAD_DIM
RWKV_W_LORA = 64
RWKV_A_LORA = 64
RWKV_G_LORA = 128
RWKV_COLS = 3 * RWKV_DIM + RWKV_W_LORA + RWKV_A_LORA + RWKV_G_LORA
RWKV_SPLITS = (RWKV_DIM, 2 * RWKV_DIM, 3 * RWKV_DIM, 3 * RWKV_DIM + RWKV_W_LORA,
               3 * RWKV_DIM + RWKV_W_LORA + RWKV_A_LORA)
GN_EPS = 64e-5
MLA_HEADS = 8
MLA_Q_LORA = 384
MLA_KV_LORA = 256
MLA_NOPE = 64
MLA_ROPE = 32
MLA_V_DIM = 64
MLA_DIM = MLA_HEADS * MLA_V_DIM
MLA_ROW = MLA_KV_LORA + MLA_ROPE
MLA_COLS = MLA_Q_LORA + MLA_KV_LORA + MLA_ROPE
MLA_SCALE = (MLA_NOPE + MLA_ROPE) ** -0.5
ROPE_THETA = 10000.0
FOX_HEADS = 8
FOX_KV_HEADS = 2
FOX_GROUP = FOX_HEADS // FOX_KV_HEADS
FOX_HEAD_DIM = 64
FOX_DIM = FOX_HEADS * FOX_HEAD_DIM
FOX_KV_DIM = FOX_KV_HEADS * FOX_HEAD_DIM
FOX_COLS = FOX_DIM + 2 * FOX_KV_DIM + FOX_HEADS
FOX_SCALE = FOX_HEAD_DIM ** -0.5
IN_COLS = RWKV_COLS + MLA_COLS + FOX_COLS + 3 * D_MODEL
IN_SPLITS = (RWKV_COLS, RWKV_COLS + MLA_COLS, RWKV_COLS + MLA_COLS + FOX_COLS,
             RWKV_COLS + MLA_COLS + FOX_COLS + D_MODEL, RWKV_COLS + MLA_COLS + FOX_COLS + 2 * D_MODEL)
D_FF = 4 * D_MODEL
Q_BLOCK = 128
RMS_EPS = 1e-6
NEG_INF = -1e30

kernel_name = 'hybrid_rwkv7_mla_fox_gated_decoder_step'


def _rmsnorm(x, g):
    x32 = x.astype(jnp.float32)
    y = x32 * lax.rsqrt(jnp.mean(x32 * x32, axis=-1, keepdims=True) + RMS_EPS)
    return y.astype(x.dtype) * g


def _rope_angles(pos):
    half = MLA_ROPE // 2
    inv = ROPE_THETA ** (-jnp.arange(half, dtype=jnp.float32) / half)
    ang = pos.astype(jnp.float32)[:, None] * inv[None, :]
    return jnp.cos(ang), jnp.sin(ang)


def _rope(x, cos, sin):
    half = x.shape[-1] // 2
    x1, x2 = x[..., :half], x[..., half:]
    cos = cos.astype(x.dtype)
    sin = sin.astype(x.dtype)
    return jnp.concatenate([x1 * cos - x2 * sin, x1 * sin + x2 * cos], axis=-1)


def _attend(q, q_pos, q_bias, segments, scale):
    logits = []
    for k, _, k_pos, k_bias in segments:
        s = jnp.einsum('bqgrd,bkgd->bgrqk', q, k, preferred_element_type=jnp.float32) * scale
        if q_bias is not None:
            qb = jnp.transpose(q_bias, (0, 2, 3, 1)).astype(jnp.float32)[..., :, None]
            kb = jnp.transpose(k_bias, (0, 2, 3, 1)).astype(jnp.float32)[..., None, :]
            s = s + (qb - kb)
        s = jnp.where(k_pos[None, :] <= q_pos[:, None], s, NEG_INF)
        logits.append(s)
    probs = jax.nn.softmax(jnp.concatenate(logits, axis=-1), axis=-1)
    outs = []
    start = 0
    for k, v, _, _ in segments:
        n = k.shape[1]
        outs.append(jnp.einsum('bgrqk,bkgd->bqgrd', probs[..., start:start + n].astype(v.dtype), v))
        start += n
    return sum(outs[1:], outs[0])


def _prompt_attention(q, q_bias, k, v, k_bias, scale):
    B, T = q.shape[:2]
    nb = T // Q_BLOCK
    k_pos = jnp.arange(T)

    def blocks(t):
        if t is None:
            return None
        return jnp.swapaxes(t.reshape((B, nb, Q_BLOCK) + t.shape[2:]), 0, 1)

    def one(xs):
        q_blk, pos_blk, bias_blk = xs
        return _attend(q_blk, pos_blk, bias_blk, [(k, v, k_pos, k_bias)], scale)

    out = lax.map(one, (blocks(q), k_pos.reshape(nb, Q_BLOCK), blocks(q_bias)))
    return jnp.swapaxes(out, 0, 1).reshape((B, T) + out.shape[3:])


def _wkv7_scan(r, w, k, v, a, b, S0):
    xs = tuple(jnp.moveaxis(t.astype(jnp.float32), 1, 0) for t in (r, w, k, v, a, b))

    def step(S, inp):
        r_t, w_t, k_t, v_t, a_t, b_t = inp
        sa = jnp.einsum('bhvk,bhk->bhv', S, a_t)
        S = S * w_t[:, :, None, :] + sa[..., None] * b_t[:, :, None, :] + v_t[..., None] * k_t[:, :, None, :]
        return S, jnp.einsum('bhvk,bhk->bhv', S, r_t)

    S, ys = lax.scan(step, S0.astype(jnp.float32), xs)
    return jnp.moveaxis(ys, 0, 1), S


def _rwkv_branch(z, prev, S0, p):
    B, T, _ = z.shape
    z_prev = jnp.concatenate([prev[:, None, :].astype(z.dtype), z[:, :-1]], axis=1)
    zs = z + (z_prev - z) * p['rwkv_mu']
    r, k, v, zw, za, zg = jnp.split(zs, RWKV_SPLITS, axis=-1)
    w = -jax.nn.softplus(-(p['rwkv_w0'] + jnp.tanh(zw) @ p['rwkv_w2'])) - 0.5
    a = jax.nn.sigmoid(p['rwkv_a0'] + za @ p['rwkv_a2'])
    g = jax.nn.sigmoid(zg) @ p['rwkv_g2']
    heads = lambda t: t.reshape(B, T, RWKV_HEADS, RWKV_HEAD_DIM)
    kk = heads(k * p['rwkv_k_k']).astype(jnp.float32)
    kk = kk / jnp.maximum(jnp.linalg.norm(kk, axis=-1, keepdims=True), 1e-12)
    k = heads(k * (1.0 + (a - 1.0) * p['rwkv_k_a']))
    r, v, a = heads(r), heads(v), heads(a)
    decay = jnp.exp(-jnp.exp(heads(w).astype(jnp.float32)))
    y, S = _wkv7_scan(r, decay, k, v, -kk, kk * a.astype(jnp.float32), S0)
    mu = jnp.mean(y, axis=-1, keepdims=True)
    var = jnp.mean(jnp.square(y - mu), axis=-1, keepdims=True)
    yn = ((y - mu) * lax.rsqrt(var + GN_EPS)).astype(r.dtype).reshape(B, T, RWKV_DIM)
    yn = yn * p['rwkv_ln_w'] + p['rwkv_ln_b']
    bonus = jnp.sum(r * k * p['rwkv_r_k'], axis=-1, keepdims=True) * v
    out = (yn + bonus.reshape(B, T, RWKV_DIM)) * g
    return out, S, z[:, -1]


def _mla_branch(z, pos, p, past):
    B, T, _ = z.shape
    c_q, c_kv, k_r = jnp.split(z, (MLA_Q_LORA, MLA_Q_LORA + MLA_KV_LORA), axis=-1)
    q = (_rmsnorm(c_q, p['mla_q_norm_g']) @ p['mla_w_uq']).reshape(B, T, MLA_HEADS, MLA_NOPE + MLA_ROPE)
    cos, sin = _rope_angles(pos)
    q_rope = _rope(q[..., MLA_NOPE:], cos[:, None, :], sin[:, None, :])
    k_r = _rope(k_r, cos, sin)
    c_kv = _rmsnorm(c_kv, p['mla_kv_norm_g'])
    rows = jnp.concatenate([c_kv, k_r], axis=-1)
    q_lat = jnp.einsum('bthd,chd->bthc', q[..., :MLA_NOPE], p['mla_w_uk'])
    q_abs = jnp.concatenate([q_lat, q_rope], axis=-1)[:, :, None]
    k_new, v_new = rows[:, :, None], c_kv[:, :, None]
    if past is None:
        o = _prompt_attention(q_abs, None, k_new, v_new, None, MLA_SCALE)
    else:
        P = past.shape[1]
        segs = [(past[:, :, None], past[:, :, None, :MLA_KV_LORA], jnp.arange(P), None),
                (k_new, v_new, pos, None)]
        o = _attend(q_abs, pos, None, segs, MLA_SCALE)
    o = jnp.einsum('btghc,chd->btghd', o, p['mla_w_uv']).reshape(B, T, MLA_DIM)
    return o, rows


def _fox_branch(z, pos, p, past):
    B, T, _ = z.shape
    q, k, v, f = jnp.split(z, (FOX_DIM, FOX_DIM + FOX_KV_DIM, FOX_DIM + 2 * FOX_KV_DIM), axis=-1)
    q = q.reshape(B, T, FOX_KV_HEADS, FOX_GROUP, FOX_HEAD_DIM)
    k = k.reshape(B, T, FOX_KV_HEADS, FOX_HEAD_DIM)
    v = v.reshape(B, T, FOX_KV_HEADS, FOX_HEAD_DIM)
    logf = jax.nn.log_sigmoid((f + p['fox_b_f']).astype(jnp.float32))
    grp = lambda c: c.reshape(B, c.shape[1], FOX_KV_HEADS, FOX_GROUP)
    if past is None:
        c = grp(jnp.cumsum(logf, axis=1))
        o = _prompt_attention(q, c, k, v, c, FOX_SCALE)
    else:
        pk, pv, plogf = past
        P = pk.shape[1]
        c_past = jnp.cumsum(plogf.astype(jnp.float32), axis=1)
        c_new = c_past[:, -1:] + jnp.cumsum(logf, axis=1)
        segs = [(pk, pv, jnp.arange(P), grp(c_past)), (k, v, pos, grp(c_new))]
        o = _attend(q, pos, grp(c_new), segs, FOX_SCALE)
    return o.reshape(B, T, FOX_DIM), k, v, logf


def _mix_sublayer(x, pos, p, rwkv_S, rwkv_prev, mla_past, fox_past):
    u = _rmsnorm(x, p['norm_mix_g'])
    z = u @ p['w_in']
    z_r, z_m, z_f, g_r, g_m, g_f = jnp.split(z, IN_SPLITS, axis=-1)
    y_r, S, prev = _rwkv_branch(z_r, rwkv_prev, rwkv_S, p)
    y_m, mla_rows = _mla_branch(z_m, pos, p, mla_past)
    y_f, fk, fv, flogf = _fox_branch(z_f, pos, p, fox_past)
    merged = (jax.nn.sigmoid(g_r) * (y_r @ p['w_branch_rwkv'])
              + jax.nn.sigmoid(g_m) * (y_m @ p['w_branch_mla'])
              + jax.nn.sigmoid(g_f) * (y_f @ p['w_branch_fox']))
    return x + merged @ p['w_out'], (mla_rows, fk, fv, flogf, S, prev)


def _ffn_sublayer(x, p):
    h = _rmsnorm(x, p['norm_ffn_g']) @ p['w_ffn_up']
    return x + jnp.square(jax.nn.relu(h)) @ p['w_ffn_down']


def setup_inputs(seed: int = 0) -> dict:
    key = jax.random.key(seed)
    ks = iter(jax.random.split(key, 64))
    nrm = lambda shape, scale=1.0: jax.random.normal(next(ks), shape, jnp.float32) * scale
    uni = lambda shape, lo, hi: jax.random.uniform(next(ks), shape, jnp.float32, lo, hi)
    n_pages = PAST_LEN // PAGE_SIZE
    n_used = DEC_BATCH * n_pages
    n_pool = n_used + max(1, n_used // 4)
    page_table = jax.random.permutation(next(ks), n_pool)[:n_used].reshape(DEC_BATCH, n_pages).astype(jnp.int32)
    D = D_MODEL
    return {
        'x_prompt': nrm((BATCH, SEQ, D)),
        'x_sample': nrm((DEC_BATCH, DEC_SEQ, D)),
        'cache_mla': nrm((DEPTH, n_pool, PAGE_SIZE, MLA_ROW)),
        'cache_fox_k': nrm((DEPTH, n_pool, PAGE_SIZE, FOX_KV_HEADS, FOX_HEAD_DIM)),
        'cache_fox_v': nrm((DEPTH, n_pool, PAGE_SIZE, FOX_KV_HEADS, FOX_HEAD_DIM)),
        'cache_fox_logf': jax.nn.log_sigmoid(nrm((DEPTH, n_pool, PAGE_SIZE, FOX_HEADS)) + 4.0),
        'state_rwkv': nrm((DEPTH, DEC_BATCH, RWKV_HEADS, RWKV_HEAD_DIM, RWKV_HEAD_DIM), 0.3),
        'state_rwkv_shift': nrm((DEPTH, DEC_BATCH, RWKV_COLS)),
        'page_table': page_table,
        'norm_mix_g': 1.0 + nrm((DEPTH, D), 0.05),
        'w_in': nrm((DEPTH, D, IN_COLS), D ** -0.5),
        'rwkv_mu': uni((DEPTH, RWKV_COLS), 0.0, 1.0),
        'rwkv_w0': uni((DEPTH, RWKV_DIM), -3.0, 0.0),
        'rwkv_w2': nrm((DEPTH, RWKV_W_LORA, RWKV_DIM), 0.1 * RWKV_W_LORA ** -0.5),
        'rwkv_a0': nrm((DEPTH, RWKV_DIM), 0.5),
        'rwkv_a2': nrm((DEPTH, RWKV_A_LORA, RWKV_DIM), 0.5 * RWKV_A_LORA ** -0.5),
        'rwkv_g2': nrm((DEPTH, RWKV_G_LORA, RWKV_DIM), RWKV_G_LORA ** -0.5),
        'rwkv_k_k': 0.85 + nrm((DEPTH, RWKV_DIM), 0.05),
        'rwkv_k_a': 1.0 + nrm((DEPTH, RWKV_DIM), 0.05),
        'rwkv_r_k': nrm((DEPTH, RWKV_HEADS, RWKV_HEAD_DIM), 0.1),
        'rwkv_ln_w': 1.0 + nrm((DEPTH, RWKV_DIM), 0.05),
        'rwkv_ln_b': nrm((DEPTH, RWKV_DIM), 0.01),
        'mla_q_norm_g': 1.0 + nrm((DEPTH, MLA_Q_LORA), 0.05),
        'mla_w_uq': nrm((DEPTH, MLA_Q_LORA, MLA_HEADS * (MLA_NOPE + MLA_ROPE)), MLA_Q_LORA ** -0.5),
        'mla_kv_norm_g': 1.0 + nrm((DEPTH, MLA_KV_LORA), 0.05),
        'mla_w_uk': nrm((DEPTH, MLA_KV_LORA, MLA_HEADS, MLA_NOPE), MLA_KV_LORA ** -0.5),
        'mla_w_uv': nrm((DEPTH, MLA_KV_LORA, MLA_HEADS, MLA_V_DIM), MLA_KV_LORA ** -0.5),
        'fox_b_f': uni((DEPTH, FOX_HEADS), 2.0, 6.0),
        'w_branch_rwkv': nrm((DEPTH, RWKV_DIM, D), RWKV_DIM ** -0.5),
        'w_branch_mla': nrm((DEPTH, MLA_DIM, D), MLA_DIM ** -0.5),
        'w_branch_fox': nrm((DEPTH, FOX_DIM, D), FOX_DIM ** -0.5),
        'w_out': nrm((DEPTH, D, D), D ** -0.5),
        'norm_ffn_g': 1.0 + nrm((DEPTH, D), 0.05),
        'w_ffn_up': nrm((DEPTH, D, D_FF), D ** -0.5),
        'w_ffn_down': nrm((DEPTH, D_FF, D), D_FF ** -0.5),
        'norm_final_g': 1.0 + nrm((D,), 0.05),
    }


def reference(x_prompt, x_sample, cache_mla, cache_fox_k, cache_fox_v, cache_fox_logf,
              state_rwkv, state_rwkv_shift, page_table, norm_mix_g, w_in, rwkv_mu, rwkv_w0,
              rwkv_w2, rwkv_a0, rwkv_a2, rwkv_g2, rwkv_k_k, rwkv_k_a, rwkv_r_k, rwkv_ln_w,
              rwkv_ln_b, mla_q_norm_g, mla_w_uq, mla_kv_norm_g, mla_w_uk, mla_w_uv, fox_b_f,
              w_branch_rwkv, w_branch_mla, w_branch_fox, w_out, norm_ffn_g, w_ffn_up,
              w_ffn_down, norm_final_g):
    B = x_prompt.shape[0]
    DB, n_new = x_sample.shape[0], x_sample.shape[1]
    past_len = page_table.shape[1] * cache_mla.shape[2]
    pos_prompt = jnp.arange(x_prompt.shape[1])
    pos_sample = past_len + jnp.arange(n_new)
    S_zero = jnp.zeros((B, RWKV_HEADS, RWKV_HEAD_DIM, RWKV_HEAD_DIM), jnp.float32)
    prev_zero = jnp.zeros((B, RWKV_COLS), x_prompt.dtype)
    xp, xs = x_prompt, x_sample
    prompt_states, sample_states = [], []
    for l in range(DEPTH):
        p = {
            'norm_mix_g': norm_mix_g[l], 'w_in': w_in[l], 'rwkv_mu': rwkv_mu[l],
            'rwkv_w0': rwkv_w0[l], 'rwkv_w2': rwkv_w2[l], 'rwkv_a0': rwkv_a0[l],
            'rwkv_a2': rwkv_a2[l], 'rwkv_g2': rwkv_g2[l], 'rwkv_k_k': rwkv_k_k[l],
            'rwkv_k_a': rwkv_k_a[l], 'rwkv_r_k': rwkv_r_k[l], 'rwkv_ln_w': rwkv_ln_w[l],
            'rwkv_ln_b': rwkv_ln_b[l], 'mla_q_norm_g': mla_q_norm_g[l], 'mla_w_uq': mla_w_uq[l],
            'mla_kv_norm_g': mla_kv_norm_g[l], 'mla_w_uk': mla_w_uk[l], 'mla_w_uv': mla_w_uv[l],
            'fox_b_f': fox_b_f[l], 'w_branch_rwkv': w_branch_rwkv[l],
            'w_branch_mla': w_branch_mla[l], 'w_branch_fox': w_branch_fox[l], 'w_out': w_out[l],
            'norm_ffn_g': norm_ffn_g[l], 'w_ffn_up': w_ffn_up[l], 'w_ffn_down': w_ffn_down[l],
        }
        xp, st_p = _mix_sublayer(xp, pos_prompt, p, S_zero, prev_zero, None, None)
        prompt_states.append(st_p)
        xp = _ffn_sublayer(xp, p)
        mla_past = cache_mla[l, page_table].reshape(DB, past_len, MLA_ROW)
        fox_past = (cache_fox_k[l, page_table].reshape(DB, past_len, FOX_KV_HEADS, FOX_HEAD_DIM),
                    cache_fox_v[l, page_table].reshape(DB, past_len, FOX_KV_HEADS, FOX_HEAD_DIM),
                    cache_fox_logf[l, page_table].reshape(DB, past_len, FOX_HEADS))
        xs, st_s = _mix_sublayer(xs, pos_sample, p, state_rwkv[l], state_rwkv_shift[l], mla_past, fox_past)
        sample_states.append(st_s)
        xs = _ffn_sublayer(xs, p)
    y_prompt = _rmsnorm(xp, norm_final_g)
    y_sample = _rmsnorm(xs, norm_final_g)
    mla_p, fk_p, fv_p, lf_p, S_p, sh_p = (jnp.stack(t) for t in zip(*prompt_states))
    mla_s, fk_s, fv_s, lf_s, S_s, sh_s = (jnp.stack(t) for t in zip(*sample_states))
    return (y_prompt, y_sample, mla_p, fk_p, fv_p, lf_p, S_p, sh_p, mla_s, fk_s, fv_s, lf_s, S_s, sh_s)
```

```python
import functools

import jax
import jax.numpy as jnp
from jax import lax
from jax.experimental import pallas as pl
from jax.experimental.pallas import tpu as pltpu

F32 = jnp.float32
BF16 = jnp.bfloat16

D_MODEL = 1024
DEPTH = 4
PAGE_SIZE = 128
RWKV_HEADS = 8
RWKV_HEAD_DIM = 64
RWKV_DIM = 512
RWKV_COLS = 1792
GN_EPS = 64e-5
WKV_CHUNK = 64
WKV_GROUP = 4
MLA_HEADS = 8
MLA_Q_LORA = 384
MLA_KV_LORA = 256
MLA_NOPE = 64
MLA_ROPE = 32
MLA_V_DIM = 64
MLA_DIM = 512
MLA_ROW = 288
MLA_SCALE = (MLA_NOPE + MLA_ROPE) ** -0.5
ROPE_THETA = 10000.0
FOX_HEADS = 8
FOX_KV_HEADS = 2
FOX_GROUP = 4
FOX_HEAD_DIM = 64
FOX_DIM = 512
FOX_SCALE = FOX_HEAD_DIM ** -0.5
D_FF = 4096
RMS_EPS = 1e-6
NEG_INF = -1e30

IN_SEGS = (
    ("z_r", 1792), ("c_q", 384), ("c_kv", 256), ("kr1", 128), ("kr2", 128),
    ("fq", 1024), ("fk", 128), ("fv", 128), ("fvs", 128), ("ff", 128), ("gates", 3072),
)
IN_COLS_P = sum(w for _, w in IN_SEGS)

VMEM_LIMIT_BYTES = 56 * 1024 * 1024


def _cparams(sem):
    return pltpu.CompilerParams(dimension_semantics=sem, vmem_limit_bytes=VMEM_LIMIT_BYTES)


def _const_spec(shape):
    n = len(shape)
    return pl.BlockSpec(shape, lambda *_: (0,) * n)


def _dot(a, b):
    return jnp.dot(a, b, preferred_element_type=F32)


def _dot_nt(a, b):
    return lax.dot_general(a, b, (((1,), (1,)), ((), ())), preferred_element_type=F32)


def _dot_tn(a, b):
    return lax.dot_general(a, b, (((0,), (0,)), ((), ())), preferred_element_type=F32)


def _dot_hi(a, b):
    return jnp.dot(a, b, preferred_element_type=F32, precision=lax.Precision.HIGHEST)


def _split_dot(x, m):
    hi = x.astype(BF16)
    lo = (x - hi.astype(F32)).astype(BF16)
    return _dot(hi, m) + _dot(lo, m)


def _rms(x, g):
    return x * lax.rsqrt(jnp.mean(x * x, axis=-1, keepdims=True) + RMS_EPS) * g


def _sigmoid(x):
    return 1.0 / (1.0 + jnp.exp(-x))


def _log_sigmoid(x):
    return -(jnp.maximum(-x, 0.0) + jnp.log(1.0 + jnp.exp(-jnp.abs(x))))


def _softplus(x):
    return jnp.maximum(x, 0.0) + jnp.log(1.0 + jnp.exp(-jnp.abs(x)))


def _in_proj_kernel(x_ref, g_ref, w_ref, *out_refs):
    ub = _rms(x_ref[...], g_ref[...]).astype(BF16)
    off = 0
    for o_ref, (_, width) in zip(out_refs, IN_SEGS):
        o_ref[...] = _dot(ub, w_ref[:, off:off + width]).astype(o_ref.dtype)
        off += width


def _in_proj(x, g, w):
    m = x.shape[0]
    tm = min(256, m)
    outs = [jax.ShapeDtypeStruct((m, wd), F32) for _, wd in IN_SEGS]
    return pl.pallas_call(
        _in_proj_kernel,
        grid=(m // tm,),
        in_specs=[pl.BlockSpec((tm, D_MODEL), lambda i: (i, 0)),
                  _const_spec((1, D_MODEL)),
                  _const_spec((D_MODEL, IN_COLS_P))],
        out_specs=[pl.BlockSpec((tm, wd), lambda i: (i, 0)) for _, wd in IN_SEGS],
        out_shape=outs,
        compiler_params=_cparams(("parallel",)),
        name="in_proj",
    )(x, g, w)


def _rwkv_prep_math(z, zprev, mu, w0, w2p, a0, a2p, g2, kk_w, ka_w, ones_bd):
    zs = z + (zprev - z) * mu
    r = zs[:, 0:512]
    k = zs[:, 512:1024]
    v = zs[:, 1024:1536]
    zwa = zs[:, 1536:1664]
    zg = zs[:, 1664:1792]
    w = -_softplus(-(w0 + _dot(jnp.tanh(zwa).astype(BF16), w2p))) - 0.5
    a = _sigmoid(a0 + _dot(zwa.astype(BF16), a2p))
    g = _dot(_sigmoid(zg).astype(BF16), g2)
    kk = k * kk_w
    ss = _split_dot(kk * kk, ones_bd)
    kkn = kk * lax.rsqrt(jnp.maximum(ss, 1e-24))
    k2 = k * (1.0 + (a - 1.0) * ka_w)
    lw = -jnp.exp(w)
    return r, lw, k2, v, -kkn, kkn * a, g


def _rwkv_prep_seq_kernel(z_ref, prev_ref, mu_ref, w0_ref, w2_ref, a0_ref, a2_ref, g2_ref,
                          kk_ref, ka_ref, ones_ref,
                          r_ref, lw_ref, k_ref, v_ref, a_ref, b_ref, g_ref, carry):
    t = pl.program_id(1)

    @pl.when(t == 0)
    def _():
        carry[...] = jnp.broadcast_to(prev_ref[0], carry.shape)

    z = z_ref[0]
    rolled = pltpu.roll(z, 1, axis=0)
    row = lax.broadcasted_iota(jnp.int32, z.shape, 0)
    zprev = jnp.where(row == 0, carry[0:1, :], rolled)
    carry[...] = jnp.broadcast_to(z[z.shape[0] - 1:, :], carry.shape)
    outs = _rwkv_prep_math(z, zprev, mu_ref[...], w0_ref[...], w2_ref[...], a0_ref[...],
                           a2_ref[...], g2_ref[...], kk_ref[...], ka_ref[...], ones_ref[...])
    for o_ref, val in zip((r_ref, lw_ref, k_ref, v_ref, a_ref, b_ref, g_ref), outs):
        o_ref[0] = val


def _rwkv_prep_tok_kernel(z_ref, zprev_ref, mu_ref, w0_ref, w2_ref, a0_ref, a2_ref, g2_ref,
                          kk_ref, ka_ref, ones_ref,
                          r_ref, lw_ref, k_ref, v_ref, a_ref, b_ref, g_ref):
    outs = _rwkv_prep_math(z_ref[...], zprev_ref[...], mu_ref[...], w0_ref[...], w2_ref[...],
                           a0_ref[...], a2_ref[...], g2_ref[...], kk_ref[...], ka_ref[...],
                           ones_ref[...])
    for o_ref, val in zip((r_ref, lw_ref, k_ref, v_ref, a_ref, b_ref, g_ref), outs):
        o_ref[...] = val


def _rwkv_param_specs():
    return [_const_spec((1, RWKV_COLS)), _const_spec((1, 512)), _const_spec((128, 512)),
            _const_spec((1, 512)), _const_spec((128, 512)), _const_spec((128, 512)),
            _const_spec((1, 512)), _const_spec((1, 512)), _const_spec((512, 512))]


def _rwkv_prep_seq(z, prev, params):
    b, t, _ = z.shape
    tt = min(256, t)
    outs = [jax.ShapeDtypeStruct((b, t, 512), F32)] * 7
    return pl.pallas_call(
        _rwkv_prep_seq_kernel,
        grid=(b, t // tt),
        in_specs=[pl.BlockSpec((1, tt, RWKV_COLS), lambda i, j: (i, j, 0)),
                  pl.BlockSpec((1, 1, RWKV_COLS), lambda i, j: (i, 0, 0))] + _rwkv_param_specs(),
        out_specs=[pl.BlockSpec((1, tt, 512), lambda i, j: (i, j, 0))] * 7,
        out_shape=outs,
        scratch_shapes=[pltpu.VMEM((8, RWKV_COLS), F32)],
        compiler_params=_cparams(("parallel", "arbitrary")),
        name="rwkv_prep_seq",
    )(z, prev, *params)


def _rwkv_prep_tok(z, zprev, params):
    n = z.shape[0]
    outs = [jax.ShapeDtypeStruct((n, 512), F32)] * 7
    return pl.pallas_call(
        _rwkv_prep_tok_kernel,
        grid=(1,),
        in_specs=[_const_spec((n, RWKV_COLS)), _const_spec((n, RWKV_COLS))] + _rwkv_param_specs(),
        out_specs=[_const_spec((n, 512))] * 7,
        out_shape=outs,
        compiler_params=_cparams(("arbitrary",)),
        name="rwkv_prep_tok",
    )(z, zprev, *params)


def _block_diag(y, bd_mask):
    yb = y.astype(BF16)
    return jnp.where(bd_mask, jnp.concatenate([yb] * WKV_GROUP, axis=0), jnp.zeros((), BF16))


def _wkv_seq_kernel(r_ref, lw_ref, k_ref, v_ref, a_ref, b_ref, y_ref, s_ref, st):
    c = pl.program_id(1)
    C = WKV_CHUNK
    W = WKV_GROUP * RWKV_HEAD_DIM

    @pl.when(c == 0)
    def _():
        st[...] = jnp.zeros_like(st)

    ri = lax.broadcasted_iota(jnp.int32, (W, W), 0)
    ci = lax.broadcasted_iota(jnp.int32, (W, W), 1)
    bd_mask = (ri // RWKV_HEAD_DIM) == (ci // RWKV_HEAD_DIM)
    tr = lax.broadcasted_iota(jnp.int32, (C, W), 0)
    tc = lax.broadcasted_iota(jnp.int32, (C, W), 1) % C
    strict = tr > tc
    incl = tr >= tc
    eye = (tr == tc).astype(F32)
    cr = lax.broadcasted_iota(jnp.int32, (C, C), 0)
    cc = lax.broadcasted_iota(jnp.int32, (C, C), 1)
    tril = (cr >= cc).astype(F32)

    for g in range(RWKV_HEADS // WKV_GROUP):
        sl = slice(g * W, (g + 1) * W)
        r, lw, k = r_ref[0, :, sl], lw_ref[0, :, sl], k_ref[0, :, sl]
        v, a, b = v_ref[0, :, sl], a_ref[0, :, sl], b_ref[0, :, sl]
        hT = st[g]
        L = _dot_hi(tril, lw)
        e_in = jnp.exp(L)
        e_neg = jnp.exp(-L)
        at = a * jnp.exp(L - lw)
        rt = r * e_in
        bt = b * e_neg
        kt = k * e_neg
        g_end = e_in[C - 1:C, :]
        lhs = jnp.concatenate([at, rt], axis=0).astype(BF16)
        gb = _dot_nt(lhs, _block_diag(bt, bd_mask))
        gk = _dot_nt(lhs, _block_diag(kt, bd_mask))
        a_ab = jnp.where(strict, gb[:C], 0.0)
        a_ak = jnp.where(strict, gk[:C], 0.0)
        a_rb = jnp.where(incl, gb[C:], 0.0)
        a_rk = jnp.where(incl, gk[C:], 0.0)
        inv = eye + a_ab
        ap = a_ab
        n = 1
        while n * 2 < C:
            ap = _dot(ap.astype(BF16), _block_diag(ap, bd_mask))
            inv = inv + _dot(inv.astype(BF16), _block_diag(ap, bd_mask))
            n *= 2
        hTb = hT.astype(BF16)
        v_bd = _block_diag(v, bd_mask)
        rhs = _dot_nt(at.astype(BF16), hTb) + _dot(a_ak.astype(BF16), v_bd)
        u = _dot(inv.astype(BF16), _block_diag(rhs, bd_mask))
        y = (_dot_nt(rt.astype(BF16), hTb) + _dot(a_rb.astype(BF16), _block_diag(u, bd_mask))
             + _dot(a_rk.astype(BF16), v_bd))
        y_ref[0, :, sl] = y
        upd = _dot_tn(u.astype(BF16), bt.astype(BF16)) + _dot_tn(v.astype(BF16), kt.astype(BF16))
        st[g] = jnp.where(bd_mask, (hT + upd) * g_end, 0.0)

    @pl.when(c == pl.num_programs(1) - 1)
    def _():
        for h in range(RWKV_HEADS):
            g, j = divmod(h, WKV_GROUP)
            lo = j * RWKV_HEAD_DIM
            s_ref[0, h] = st[g, lo:lo + RWKV_HEAD_DIM, lo:lo + RWKV_HEAD_DIM]


def _wkv_seq(r, lw, k, v, a, b):
    bsz, t, _ = r.shape
    spec = pl.BlockSpec((1, WKV_CHUNK, 512), lambda i, j: (i, j, 0))
    w = WKV_GROUP * RWKV_HEAD_DIM
    return pl.pallas_call(
        _wkv_seq_kernel,
        grid=(bsz, t // WKV_CHUNK),
        in_specs=[spec] * 6,
        out_specs=[spec, pl.BlockSpec((1, RWKV_HEADS, 64, 64), lambda i, j: (i, 0, 0, 0))],
        out_shape=[jax.ShapeDtypeStruct((bsz, t, 512), F32),
                   jax.ShapeDtypeStruct((bsz, RWKV_HEADS, 64, 64), F32)],
        scratch_shapes=[pltpu.VMEM((RWKV_HEADS // WKV_GROUP, w, w), F32)],
        compiler_params=_cparams(("parallel", "arbitrary")),
        name="wkv_seq",
    )(r, lw, k, v, a, b)


def _wkv_step_kernel(s_ref, r_ref, lw_ref, k_ref, v_ref, a_ref, b_ref, y_ref, so_ref):
    s = s_ref[...]
    eye = (lax.broadcasted_iota(jnp.int32, (64, 64), 0)
           == lax.broadcasted_iota(jnp.int32, (64, 64), 1))[None, None]
    row = lambda ref: ref[...]
    w = jnp.exp(row(lw_ref))
    vcol = jnp.sum(jnp.where(eye, row(v_ref), 0.0), axis=-1, keepdims=True)
    sa = jnp.sum(s * row(a_ref), axis=-1, keepdims=True)
    s = s * w + sa * row(b_ref) + vcol * row(k_ref)
    ycol = jnp.sum(s * row(r_ref), axis=-1, keepdims=True)
    y_ref[...] = jnp.sum(jnp.where(eye, ycol, 0.0), axis=-2, keepdims=True)
    so_ref[...] = s


def _wkv_step(s, r, lw, k, v, a, b):
    n = s.shape[0]
    nb = 8
    vspec = pl.BlockSpec((nb, RWKV_HEADS, 1, 64), lambda i: (i, 0, 0, 0))
    sspec = pl.BlockSpec((nb, RWKV_HEADS, 64, 64), lambda i: (i, 0, 0, 0))
    return pl.pallas_call(
        _wkv_step_kernel,
        grid=(n // nb,),
        in_specs=[sspec] + [vspec] * 6,
        out_specs=[vspec, sspec],
        out_shape=[jax.ShapeDtypeStruct((n, RWKV_HEADS, 1, 64), F32),
                   jax.ShapeDtypeStruct(s.shape, F32)],
        compiler_params=_cparams(("parallel",)),
        name="wkv_step",
    )(s, r, lw, k, v, a, b)


def _rwkv_post_kernel(y_ref, r_ref, k_ref, v_ref, g_ref, lnw_ref, lnb_ref, rk_ref, ones_ref,
                      o_ref):
    y = y_ref[...]
    ones = ones_ref[...]
    inv_n = 1.0 / RWKV_HEAD_DIM
    mu = _split_dot(y, ones) * inv_n
    d = y - mu
    var = _split_dot(d * d, ones) * inv_n
    yn = d * lax.rsqrt(var + GN_EPS) * lnw_ref[...] + lnb_ref[...]
    bonus = _split_dot(r_ref[...] * k_ref[...] * rk_ref[...], ones) * v_ref[...]
    o_ref[...] = ((yn + bonus) * g_ref[...]).astype(o_ref.dtype)


def _rwkv_post(y, r, k, v, g, lnw, lnb, rk, ones_bd):
    m = y.shape[0]
    tm = min(512, m)
    spec = pl.BlockSpec((tm, 512), lambda i: (i, 0))
    return pl.pallas_call(
        _rwkv_post_kernel,
        grid=(m // tm,),
        in_specs=[spec] * 5 + [_const_spec((1, 512))] * 3 + [_const_spec((512, 512))],
        out_specs=spec,
        out_shape=jax.ShapeDtypeStruct((m, 512), BF16),
        compiler_params=_cparams(("parallel",)),
        name="rwkv_post",
    )(y, r, k, v, g, lnw, lnb, rk, ones_bd)


def _mla_prep_kernel(zq_ref, zc_ref, zk1_ref, zk2_ref, cos_ref, sin_ref, gq_ref, wn_ref, wr1_ref,
                     wr2_ref, gkv_ref, wuk_ref, qlat_ref, qr_ref, rows_ref, ckv_ref, krt_ref):
    cos, sin = cos_ref[...], sin_ref[...]
    cq = _rms(zq_ref[0], gq_ref[...]).astype(BF16)
    qn = _dot(cq, wn_ref[...]) * MLA_SCALE
    q1 = _dot(cq, wr1_ref[...]) * MLA_SCALE
    q2 = _dot(cq, wr2_ref[...]) * MLA_SCALE
    qr_ref[0] = jnp.concatenate([q1 * cos - q2 * sin, q1 * sin + q2 * cos], axis=-1).astype(BF16)
    for p in range(MLA_HEADS // 2):
        qlat_ref[0, :, 512 * p:512 * (p + 1)] = _dot(
            qn[:, 128 * p:128 * (p + 1)].astype(BF16), wuk_ref[p]).astype(BF16)
    ckv = _rms(zc_ref[0], gkv_ref[...])
    z1, z2 = zk1_ref[0], zk2_ref[0]
    k1 = z1 * cos - z2 * sin
    k2 = z1 * sin + z2 * cos
    rows_ref[0, :, 0:256] = ckv
    rows_ref[0, :, 256:272] = k1[:, 0:16]
    rows_ref[0, :, 272:288] = k2[:, 0:16]
    ckv_ref[0] = ckv.astype(BF16)
    krt_ref[0] = jnp.concatenate([k1, k2], axis=-1).astype(BF16)


def _mla_prep(zq, zc, zk1, zk2, cos8, sin8, p):
    b, t, _ = zq.shape
    tt = min(256, t)
    io = lambda wd: pl.BlockSpec((1, tt, wd), lambda i, j: (i, j, 0))
    tab = pl.BlockSpec((tt, 128), lambda i, j: (j, 0))
    return pl.pallas_call(
        _mla_prep_kernel,
        grid=(b, t // tt),
        in_specs=[io(384), io(256), io(128), io(128), tab, tab,
                  _const_spec((1, 384)), _const_spec((384, 512)), _const_spec((384, 128)),
                  _const_spec((384, 128)), _const_spec((1, 256)), _const_spec((4, 128, 512))],
        out_specs=[io(2048), io(256), io(MLA_ROW), io(256), io(256)],
        out_shape=[jax.ShapeDtypeStruct((b, t, 2048), BF16),
                   jax.ShapeDtypeStruct((b, t, 256), BF16),
                   jax.ShapeDtypeStruct((b, t, MLA_ROW), F32),
                   jax.ShapeDtypeStruct((b, t, 256), BF16),
                   jax.ShapeDtypeStruct((b, t, 256), BF16)],
        compiler_params=_cparams(("parallel", "parallel")),
        name="mla_prep",
    )(zq, zc, zk1, zk2, cos8, sin8, p["gq"], p["wq_n"], p["wq_r1"], p["wq_r2"], p["gkv"],
      p["wuk"])


def _rope_lane_mask(h, shape):
    lane = lax.broadcasted_iota(jnp.int32, shape, len(shape) - 1) % 128
    return (lane >= 16 * h) & (lane < 16 * (h + 1))


def _mla_attn_kernel(qlat_ref, qr_ref, ck_ref, kr_ref, wuv_ref, o_ref, m_scr, l_scr, acc_scr):
    qi, kj = pl.program_id(1), pl.program_id(2)
    tq, tk = qr_ref.shape[1], ck_ref.shape[1]

    @pl.when(kj == 0)
    def _():
        m_scr[...] = jnp.full_like(m_scr, NEG_INF)
        l_scr[...] = jnp.zeros_like(l_scr)
        acc_scr[...] = jnp.zeros_like(acc_scr)

    @pl.when(kj <= qi)
    def _():
        ck, kr, qr = ck_ref[0], kr_ref[0], qr_ref[0]
        qpos = qi * tq + lax.broadcasted_iota(jnp.int32, (tq, tk), 0)
        kpos = kj * tk + lax.broadcasted_iota(jnp.int32, (tq, tk), 1)
        causal = kpos <= qpos
        for h in range(MLA_HEADS):
            qrh = jnp.where(_rope_lane_mask(h, qr.shape), qr, jnp.zeros((), BF16))
            s = _dot_nt(qlat_ref[0, :, 256 * h:256 * (h + 1)], ck) + _dot_nt(qrh, kr)
            s = jnp.where(causal, s, NEG_INF)
            m_prev = m_scr[h]
            m_new = jnp.maximum(m_prev, jnp.max(s, axis=-1, keepdims=True))
            alpha = jnp.exp(m_prev - m_new)
            p = jnp.exp(s - m_new)
            l_scr[h] = alpha * l_scr[h] + jnp.sum(p, axis=-1, keepdims=True)
            acc_scr[h] = alpha * acc_scr[h] + _dot(p.astype(BF16), ck)
            m_scr[h] = m_new

    @pl.when(kj == qi)
    def _():
        o = jnp.concatenate([(acc_scr[h] / l_scr[h]).astype(BF16) for h in range(MLA_HEADS)],
                            axis=-1)
        o_ref[0] = _dot(o, wuv_ref[...]).astype(o_ref.dtype)


def _mla_attn(qlat, qr, ckv, krt, wuv):
    b, t, _ = qr.shape
    tq = tk = min(256, t)
    n = t // tq
    kspec = pl.BlockSpec((1, tk, 256), lambda i, q, k: (i, jnp.minimum(k, q), 0))
    return pl.pallas_call(
        _mla_attn_kernel,
        grid=(b, n, n),
        in_specs=[pl.BlockSpec((1, tq, 2048), lambda i, q, k: (i, q, 0)),
                  pl.BlockSpec((1, tq, 256), lambda i, q, k: (i, q, 0)),
                  kspec, kspec, _const_spec((2048, 512))],
        out_specs=pl.BlockSpec((1, tq, 512), lambda i, q, k: (i, q, 0)),
        out_shape=jax.ShapeDtypeStruct((b, t, 512), BF16),
        scratch_shapes=[pltpu.VMEM((MLA_HEADS, tq, 1), F32), pltpu.VMEM((MLA_HEADS, tq, 1), F32),
                        pltpu.VMEM((MLA_HEADS, tq, 256), F32)],
        compiler_params=_cparams(("parallel", "parallel", "arbitrary")),
        name="mla_attn",
    )(qlat, qr, ckv, krt, wuv)


def _fox_prep_kernel(ff_ref, bf_ref, lf_ref, c_ref, ct_ref, carry_r, carry_c):
    t = pl.program_id(1)
    tt = ff_ref.shape[1]

    @pl.when(t == 0)
    def _():
        carry_r[...] = jnp.zeros_like(carry_r)
        carry_c[...] = jnp.zeros_like(carry_c)

    lane = lax.broadcasted_iota(jnp.int32, (tt, 128), 1)
    lf = jnp.where(lane < FOX_HEADS, _log_sigmoid(ff_ref[0] + bf_ref[...]), 0.0)
    lf_ref[0] = lf[:, 0:FOX_HEADS]
    ri = lax.broadcasted_iota(jnp.int32, (tt, tt), 0)
    ci = lax.broadcasted_iota(jnp.int32, (tt, tt), 1)
    c = _dot_hi((ri >= ci).astype(F32), lf) + carry_r[0:1, :]
    ct = lax.dot_general(lf, (ri <= ci).astype(F32), (((0,), (0,)), ((), ())),
                         preferred_element_type=F32,
                         precision=lax.Precision.HIGHEST) + carry_c[:, 0:1]
    c_ref[0] = c
    ct_ref[0] = ct[0:FOX_HEADS, :]
    carry_r[...] = jnp.broadcast_to(c[tt - 1:tt, :], carry_r.shape)
    carry_c[...] = jnp.broadcast_to(ct[:, tt - 1:tt], carry_c.shape)


def _fox_prep(ff, bf):
    b, t, _ = ff.shape
    tt = min(256, t)
    return pl.pallas_call(
        _fox_prep_kernel,
        grid=(b, t // tt),
        in_specs=[pl.BlockSpec((1, tt, 128), lambda i, j: (i, j, 0)), _const_spec((1, 128))],
        out_specs=[pl.BlockSpec((1, tt, FOX_HEADS), lambda i, j: (i, j, 0)),
                   pl.BlockSpec((1, tt, 128), lambda i, j: (i, j, 0)),
                   pl.BlockSpec((1, FOX_HEADS, tt), lambda i, j: (i, 0, j))],
        out_shape=[jax.ShapeDtypeStruct((b, t, FOX_HEADS), F32),
                   jax.ShapeDtypeStruct((b, t, 128), F32),
                   jax.ShapeDtypeStruct((b, FOX_HEADS, t), F32)],
        scratch_shapes=[pltpu.VMEM((8, 128), F32), pltpu.VMEM((128, 128), F32)],
        compiler_params=_cparams(("parallel", "arbitrary")),
        name="fox_prep",
    )(ff, bf)


def _fox_attn_kernel(q_ref, k_ref, v_ref, vs_ref, c_ref, ct_ref, o_ref, m_scr, l_scr, acc_scr):
    qi, kj = pl.program_id(1), pl.program_id(2)
    tq, tk = q_ref.shape[1], k_ref.shape[1]

    @pl.when(kj == 0)
    def _():
        m_scr[...] = jnp.full_like(m_scr, NEG_INF)
        l_scr[...] = jnp.zeros_like(l_scr)
        acc_scr[...] = jnp.zeros_like(acc_scr)

    @pl.when(kj <= qi)
    def _():
        k = k_ref[0].astype(BF16)
        v = v_ref[0].astype(BF16)
        vs = vs_ref[0].astype(BF16)
        cq = c_ref[0]
        ck = ct_ref[0]
        qpos = qi * tq + lax.broadcasted_iota(jnp.int32, (tq, tk), 0)
        kpos = kj * tk + lax.broadcasted_iota(jnp.int32, (tq, tk), 1)
        causal = kpos <= qpos
        for h in range(FOX_HEADS):
            q = q_ref[0, :, 128 * h:128 * (h + 1)].astype(BF16)
            s = _dot_nt(q, k) + (cq[:, h:h + 1] - ck[h:h + 1, :])
            s = jnp.where(causal, s, NEG_INF)
            m_prev = m_scr[h]
            m_new = jnp.maximum(m_prev, jnp.max(s, axis=-1, keepdims=True))
            alpha = jnp.exp(m_prev - m_new)
            p = jnp.exp(s - m_new)
            l_scr[h] = alpha * l_scr[h] + jnp.sum(p, axis=-1, keepdims=True)
            vv = v if (h % 2 == 0) == (h // FOX_GROUP == 0) else vs
            acc_scr[h] = alpha * acc_scr[h] + _dot(p.astype(BF16), vv)
            m_scr[h] = m_new

    @pl.when(kj == qi)
    def _():
        lane = lax.broadcasted_iota(jnp.int32, (tq, 128), 1)
        for p2 in range(FOX_HEADS // 2):
            even = acc_scr[2 * p2] / l_scr[2 * p2]
            odd = acc_scr[2 * p2 + 1] / l_scr[2 * p2 + 1]
            o_ref[0, :, 128 * p2:128 * (p2 + 1)] = jnp.where(lane < 64, even, odd).astype(o_ref.dtype)


def _fox_attn(q, k, v, vs, c, ct):
    b, t, _ = k.shape
    tq = tk = min(256, t)
    n = t // tq
    kspec = pl.BlockSpec((1, tk, 128), lambda i, qq, kk: (i, jnp.minimum(kk, qq), 0))
    return pl.pallas_call(
        _fox_attn_kernel,
        grid=(b, n, n),
        in_specs=[pl.BlockSpec((1, tq, 1024), lambda i, qq, kk: (i, qq, 0)),
                  kspec, kspec, kspec,
                  pl.BlockSpec((1, tq, 128), lambda i, qq, kk: (i, qq, 0)),
                  pl.BlockSpec((1, FOX_HEADS, tk), lambda i, qq, kk: (i, 0, jnp.minimum(kk, qq)))],
        out_specs=pl.BlockSpec((1, tq, 512), lambda i, qq, kk: (i, qq, 0)),
        out_shape=jax.ShapeDtypeStruct((b, t, 512), BF16),
        scratch_shapes=[pltpu.VMEM((FOX_HEADS, tq, 1), F32), pltpu.VMEM((FOX_HEADS, tq, 1), F32),
                        pltpu.VMEM((FOX_HEADS, tq, 128), F32)],
        compiler_params=_cparams(("parallel", "parallel", "arbitrary")),
        name="fox_attn",
    )(q, k, v, vs, c, ct)


DEC_PAGES = 32
DEC_BLOCK = 256


def _page_copies(pt_ref, seq, chunk, slot, layer, pairs, sem):
    out = []
    for j in range(DEC_PAGES):
        page = pt_ref[seq, chunk * DEC_PAGES + j]
        for cache_ref, buf_ref in pairs:
            out.append(pltpu.make_async_copy(cache_ref.at[layer, page], buf_ref.at[slot, j],
                                             sem.at[slot]))
    return out


def _paged_pipeline(pt_ref, pairs, sem, layer):
    n_chunks = pl.num_programs(1)
    step = pl.program_id(0) * n_chunks + pl.program_id(1)
    total = pl.num_programs(0) * n_chunks
    slot = step % 2

    @pl.when(step == 0)
    def _():
        for cp in _page_copies(pt_ref, 0, 0, 0, layer, pairs, sem):
            cp.start()

    nxt = step + 1

    @pl.when(nxt < total)
    def _():
        for cp in _page_copies(pt_ref, nxt // n_chunks, nxt % n_chunks, 1 - slot, layer, pairs, sem):
            cp.start()

    for cp in _page_copies(pt_ref, pl.program_id(0), pl.program_id(1), slot, layer, pairs, sem):
        cp.wait()
    return slot


def _mla_dec_kernel(pt_ref, q_ref, new_ref, cache_ref, o_ref, buf, sem, m_scr, l_scr, acc_scr, *,
                    layer):
    c = pl.program_id(1)
    slot = _paged_pipeline(pt_ref, [(cache_ref, buf)], sem, layer)

    @pl.when(c == 0)
    def _():
        m_scr[...] = jnp.full_like(m_scr, NEG_INF)
        l_scr[...] = jnp.zeros_like(l_scr)
        acc_scr[...] = jnp.zeros_like(acc_scr)

    q = q_ref[0]
    m, l, acc = m_scr[...], l_scr[...], acc_scr[...]
    for blk in range(DEC_PAGES * PAGE_SIZE // DEC_BLOCK):
        ppb = DEC_BLOCK // PAGE_SIZE
        kv = jnp.concatenate([buf[slot, blk * ppb + j] for j in range(ppb)], axis=0).astype(BF16)
        s = _dot_nt(q, kv)
        m_new = jnp.maximum(m, jnp.max(s, axis=-1, keepdims=True))
        alpha = jnp.exp(m - m_new)
        p = jnp.exp(s - m_new)
        l = alpha * l + jnp.sum(p, axis=-1, keepdims=True)
        acc = alpha * acc + _dot(p.astype(BF16), kv[:, 0:MLA_KV_LORA])
        m = m_new
    m_scr[...], l_scr[...], acc_scr[...] = m, l, acc

    @pl.when(c == pl.num_programs(1) - 1)
    def _():
        new = new_ref[0].astype(BF16).astype(F32)
        s_new = jnp.sum(q.astype(F32) * new, axis=-1, keepdims=True)
        m_fin = jnp.maximum(m, s_new)
        w_old = jnp.exp(m - m_fin)
        p_new = jnp.exp(s_new - m_fin).astype(BF16).astype(F32)
        den = l * w_old + p_new
        o_ref[0] = ((acc * w_old + p_new * new[:, 0:MLA_KV_LORA]) / den).astype(o_ref.dtype)


def _mla_decode(page_table, q, new_rows, cache, layer):
    n, n_pages = page_table.shape
    grid_spec = pltpu.PrefetchScalarGridSpec(
        num_scalar_prefetch=1,
        grid=(n, n_pages // DEC_PAGES),
        in_specs=[pl.BlockSpec((1, MLA_HEADS, MLA_ROW), lambda i, c, pt: (i, 0, 0)),
                  pl.BlockSpec((1, 1, MLA_ROW), lambda i, c, pt: (i, 0, 0)),
                  pl.BlockSpec(memory_space=pl.ANY)],
        out_specs=pl.BlockSpec((1, MLA_HEADS, MLA_KV_LORA), lambda i, c, pt: (i, 0, 0)),
        scratch_shapes=[pltpu.VMEM((2, DEC_PAGES, PAGE_SIZE, MLA_ROW), F32),
                        pltpu.SemaphoreType.DMA((2,)),
                        pltpu.VMEM((MLA_HEADS, 1), F32), pltpu.VMEM((MLA_HEADS, 1), F32),
                        pltpu.VMEM((MLA_HEADS, MLA_KV_LORA), F32)],
    )
    return pl.pallas_call(
        functools.partial(_mla_dec_kernel, layer=layer),
        grid_spec=grid_spec,
        out_shape=jax.ShapeDtypeStruct((n, MLA_HEADS, MLA_KV_LORA), BF16),
        compiler_params=_cparams(("arbitrary", "arbitrary")),
        name="mla_decode",
    )(page_table, q, new_rows, cache)


def _fox_dec_kernel(pt_ref, q_ref, kn_ref, vn_ref, ffn_ref, bf_ref, kc_ref, vc_ref, lc_ref,
                    o_ref, lf_ref, kbuf, vbuf, lbuf, sem, m_scr, l_scr, acc_scr, run_scr, *, layer):
    c = pl.program_id(1)
    slot = _paged_pipeline(pt_ref, [(kc_ref, kbuf), (vc_ref, vbuf), (lc_ref, lbuf)], sem, layer)
    H = FOX_HEADS
    nblk = DEC_PAGES * PAGE_SIZE // DEC_BLOCK
    ppb = DEC_BLOCK // PAGE_SIZE
    R = nblk * H

    @pl.when(c == 0)
    def _():
        m_scr[...] = jnp.full_like(m_scr, NEG_INF)
        l_scr[...] = jnp.zeros_like(l_scr)
        acc_scr[...] = jnp.zeros_like(acc_scr)
        run_scr[...] = jnp.zeros_like(run_scr)

    q = q_ref[0]
    lf = jnp.concatenate(
        [jnp.concatenate([lbuf[slot, blk * ppb + j] for j in range(ppb)], axis=1)
         for blk in range(nblk)], axis=0)
    ri = lax.broadcasted_iota(jnp.int32, (DEC_BLOCK, DEC_BLOCK), 0)
    ci = lax.broadcasted_iota(jnp.int32, (DEC_BLOCK, DEC_BLOCK), 1)
    cum = _dot_hi(lf, (ri <= ci).astype(F32))
    rr = lax.broadcasted_iota(jnp.int32, (R, R), 0)
    rc = lax.broadcasted_iota(jnp.int32, (R, R), 1)
    same_head = (rr % H) == (rc % H)
    tot = cum[:, DEC_BLOCK - 1:DEC_BLOCK]
    tot_b = jnp.broadcast_to(tot, (R, 128))
    before = _dot_hi((same_head & (rc // H < rr // H)).astype(F32), tot_b)
    whole = _dot_hi(same_head.astype(F32), tot_b)
    run = run_scr[...]
    cpast = cum + before[:, 0:1] + run[:, 0:1]
    run_scr[...] = run + whole

    s = jnp.concatenate(
        [_dot_nt(q, jnp.concatenate([kbuf[slot, blk * ppb + j] for j in range(ppb)],
                                    axis=0).astype(BF16)) for blk in range(nblk)], axis=0)
    x = (s - cpast).reshape(nblk, H, DEC_BLOCK)
    m = m_scr[...]
    m_new = jnp.maximum(m, jnp.max(jnp.max(x, axis=-1, keepdims=True), axis=0))
    alpha = jnp.exp(m - m_new)
    p = jnp.exp(x - m_new[None])
    l_scr[...] = alpha * l_scr[...] + jnp.sum(jnp.sum(p, axis=-1, keepdims=True), axis=0)
    acc = alpha * acc_scr[...]
    for blk in range(nblk):
        vb = jnp.concatenate([vbuf[slot, blk * ppb + j] for j in range(ppb)], axis=0).astype(BF16)
        acc = acc + _dot(p[blk].astype(BF16), vb)
    acc_scr[...] = acc
    m_scr[...] = m_new

    @pl.when(c == pl.num_programs(1) - 1)
    def _():
        diag = (lax.broadcasted_iota(jnp.int32, (H, 128), 0)
                == lax.broadcasted_iota(jnp.int32, (H, 128), 1))
        lf_new_row = _log_sigmoid(ffn_ref[0] + bf_ref[...])
        lf_ref[0] = lf_new_row[:, 0:H]
        lf_new = jnp.sum(jnp.where(diag, lf_new_row, 0.0), axis=-1, keepdims=True)
        c_new = run_scr[0:H, 0:1] + lf_new
        kn = kn_ref[0].astype(BF16).astype(F32)
        vn = vn_ref[0].astype(BF16).astype(F32)
        s_new = jnp.sum(q.astype(F32) * kn, axis=-1, keepdims=True) + (c_new - c_new)
        m_past = m_scr[...] + c_new
        m_fin = jnp.maximum(m_past, s_new)
        w_old = jnp.exp(m_past - m_fin)
        p_new = jnp.exp(s_new - m_fin).astype(BF16).astype(F32)
        den = l_scr[...] * w_old + p_new
        o = (acc_scr[...] * w_old + p_new * vn) / den
        head = lax.broadcasted_iota(jnp.int32, (H, 64), 0)
        o_ref[0] = jnp.where(head < FOX_GROUP, o[:, 0:64], o[:, 64:128]).astype(o_ref.dtype)


def _fox_decode(page_table, q, k_new, v_new, ff_new, bf, cache_k, cache_v, cache_lt, layer):
    n, n_pages = page_table.shape
    row = lambda wd: pl.BlockSpec((1, 1, wd), lambda i, c, pt: (i, 0, 0))
    rows = DEC_PAGES * PAGE_SIZE // DEC_BLOCK * FOX_HEADS
    grid_spec = pltpu.PrefetchScalarGridSpec(
        num_scalar_prefetch=1,
        grid=(n, n_pages // DEC_PAGES),
        in_specs=[pl.BlockSpec((1, FOX_HEADS, 128), lambda i, c, pt: (i, 0, 0)),
                  row(128), row(128), row(128),
                  pl.BlockSpec((1, 128), lambda i, c, pt: (0, 0)),
                  pl.BlockSpec(memory_space=pl.ANY), pl.BlockSpec(memory_space=pl.ANY),
                  pl.BlockSpec(memory_space=pl.ANY)],
        out_specs=[pl.BlockSpec((1, FOX_HEADS, 64), lambda i, c, pt: (i, 0, 0)),
                   pl.BlockSpec((1, 1, FOX_HEADS), lambda i, c, pt: (i, 0, 0))],
        scratch_shapes=[pltpu.VMEM((2, DEC_PAGES, PAGE_SIZE, 128), F32),
                        pltpu.VMEM((2, DEC_PAGES, PAGE_SIZE, 128), F32),
                        pltpu.VMEM((2, DEC_PAGES, FOX_HEADS, PAGE_SIZE), F32),
                        pltpu.SemaphoreType.DMA((2,)),
                        pltpu.VMEM((FOX_HEADS, 1), F32), pltpu.VMEM((FOX_HEADS, 1), F32),
                        pltpu.VMEM((FOX_HEADS, 128), F32), pltpu.VMEM((rows, 128), F32)],
    )
    return pl.pallas_call(
        functools.partial(_fox_dec_kernel, layer=layer),
        grid_spec=grid_spec,
        out_shape=[jax.ShapeDtypeStruct((n, FOX_HEADS, 64), BF16),
                   jax.ShapeDtypeStruct((n, 1, FOX_HEADS), F32)],
        compiler_params=_cparams(("arbitrary", "arbitrary")),
        name="fox_decode",
    )(page_table, q, k_new, v_new, ff_new, bf, cache_k, cache_v, cache_lt)


def _matmul_kernel(x_ref, w_ref, o_ref):
    o_ref[...] = _dot(x_ref[...], w_ref[...]).astype(o_ref.dtype)


def _matmul(x, w, out_dtype):
    m, k = x.shape
    n = w.shape[1]
    tm = min(256, m)
    return pl.pallas_call(
        _matmul_kernel,
        grid=(m // tm,),
        in_specs=[pl.BlockSpec((tm, k), lambda i: (i, 0)), _const_spec((k, n))],
        out_specs=pl.BlockSpec((tm, n), lambda i: (i, 0)),
        out_shape=jax.ShapeDtypeStruct((m, n), out_dtype),
        compiler_params=_cparams(("parallel",)),
        name="matmul",
    )(x, w)


def _merge_kernel(x_ref, yr_ref, ym_ref, yf_ref, g_ref, wr_ref, wm_ref, wf_ref, wo_ref, o_ref):
    merged = (_sigmoid(g_ref[:, 0:1024]) * _dot(yr_ref[...], wr_ref[...])
              + _sigmoid(g_ref[:, 1024:2048]) * _dot(ym_ref[...], wm_ref[...])
              + _sigmoid(g_ref[:, 2048:3072]) * _dot(yf_ref[...], wf_ref[...]))
    o_ref[...] = x_ref[...] + _dot(merged.astype(BF16), wo_ref[...])


def _merge(x, yr, ym, yf, gates, wr, wm, wf, wo):
    m = x.shape[0]
    tm = min(256, m)
    row = lambda wd: pl.BlockSpec((tm, wd), lambda i: (i, 0))
    return pl.pallas_call(
        _merge_kernel,
        grid=(m // tm,),
        in_specs=[row(D_MODEL), row(512), row(512), row(512), row(3072),
                  _const_spec((512, D_MODEL)), _const_spec((512, D_MODEL)),
                  _const_spec((512, D_MODEL)), _const_spec((D_MODEL, D_MODEL))],
        out_specs=row(D_MODEL),
        out_shape=jax.ShapeDtypeStruct((m, D_MODEL), F32),
        compiler_params=_cparams(("parallel",)),
        name="merge",
    )(x, yr, ym, yf, gates, wr, wm, wf, wo)


FFN_CHUNK = 1024


def _ffn_kernel(x_ref, g_ref, wu_ref, wd_ref, o_ref):
    x = x_ref[...]
    ub = _rms(x, g_ref[...]).astype(BF16)
    acc = x
    for c in range(D_FF // FFN_CHUNK):
        sl = slice(c * FFN_CHUNK, (c + 1) * FFN_CHUNK)
        h = jnp.maximum(_dot(ub, wu_ref[:, sl]), 0.0)
        acc = acc + _dot((h * h).astype(BF16), wd_ref[sl, :])
    o_ref[...] = acc


def _ffn(x, g, wu, wd):
    m = x.shape[0]
    tm = min(256, m)
    row = pl.BlockSpec((tm, D_MODEL), lambda i: (i, 0))
    return pl.pallas_call(
        _ffn_kernel,
        grid=(m // tm,),
        in_specs=[row, _const_spec((1, D_MODEL)), _const_spec((D_MODEL, D_FF)),
                  _const_spec((D_FF, D_MODEL))],
        out_specs=row,
        out_shape=jax.ShapeDtypeStruct((m, D_MODEL), F32),
        compiler_params=_cparams(("parallel",)),
        name="ffn",
    )(x, g, wu, wd)


def _final_norm_kernel(x_ref, g_ref, o_ref):
    o_ref[...] = _rms(x_ref[...], g_ref[...])


def _final_norm(x, g):
    m = x.shape[0]
    tm = min(512, m)
    row = pl.BlockSpec((tm, D_MODEL), lambda i: (i, 0))
    return pl.pallas_call(
        _final_norm_kernel,
        grid=(m // tm,),
        in_specs=[row, _const_spec((1, D_MODEL))],
        out_specs=row,
        out_shape=jax.ShapeDtypeStruct((m, D_MODEL), F32),
        compiler_params=_cparams(("parallel",)),
        name="final_norm",
    )(x, g)


def _prep_weights(w_in, rwkv_mu, rwkv_w0, rwkv_w2, rwkv_a0, rwkv_a2, rwkv_g2, rwkv_k_k, rwkv_k_a,
                  rwkv_r_k, rwkv_ln_w, rwkv_ln_b, mla_q_norm_g, mla_w_uq, mla_kv_norm_g, mla_w_uk,
                  mla_w_uv, fox_b_f, w_branch_rwkv, w_branch_mla, w_branch_fox, w_out, w_ffn_up,
                  w_ffn_down):
    L = w_in.shape[0]
    o_m = RWKV_COLS
    o_kr = o_m + MLA_Q_LORA + MLA_KV_LORA
    o_f = o_kr + MLA_ROPE
    o_fk = o_f + FOX_DIM
    o_fv = o_fk + 128
    o_ff = o_fv + 128
    o_g = o_ff + FOX_HEADS
    zeros = lambda n: jnp.zeros((L, D_MODEL, n), w_in.dtype)
    half = MLA_ROPE // 2
    fq = w_in[:, :, o_f:o_fk].reshape(L, D_MODEL, FOX_HEADS, 64) * FOX_SCALE
    fq_tiles = []
    for h in range(FOX_HEADS):
        parts = [fq[:, :, h], zeros(64)]
        fq_tiles.append(jnp.concatenate(parts if h // FOX_GROUP == 0 else parts[::-1], axis=-1))
    fv = w_in[:, :, o_fv:o_ff]
    w_in_p = jnp.concatenate(
        [w_in[:, :, 0:o_kr],
         jnp.tile(w_in[:, :, o_kr:o_kr + half], (1, 1, 8)),
         jnp.tile(w_in[:, :, o_kr + half:o_f], (1, 1, 8))]
        + fq_tiles
        + [w_in[:, :, o_fk:o_fv], fv, jnp.concatenate([fv[:, :, 64:], fv[:, :, :64]], axis=-1),
           w_in[:, :, o_ff:o_g], zeros(128 - FOX_HEADS), w_in[:, :, o_g:]], axis=-1).astype(BF16)
    assert w_in_p.shape[-1] == IN_COLS_P

    pad_rows = lambda w, top: jnp.concatenate(
        [jnp.zeros_like(w), w] if top else [w, jnp.zeros_like(w)], axis=1).astype(BF16)
    row = lambda v: v.reshape(L, 1, -1)
    idx = jnp.arange(512)
    ones_bd = ((idx[:, None] // 64) == (idx[None, :] // 64)).astype(BF16)
    rwkv = dict(mu=row(rwkv_mu), w0=row(rwkv_w0), w2=pad_rows(rwkv_w2, False), a0=row(rwkv_a0),
                a2=pad_rows(rwkv_a2, True), g2=rwkv_g2.astype(BF16), kk=row(rwkv_k_k),
                ka=row(rwkv_k_a), rk=row(rwkv_r_k), lnw=row(rwkv_ln_w), lnb=row(rwkv_ln_b),
                ones=ones_bd)

    uq = mla_w_uq.reshape(L, MLA_Q_LORA, MLA_HEADS, MLA_NOPE + MLA_ROPE)
    wq_n = uq[..., :MLA_NOPE].reshape(L, MLA_Q_LORA, 512)
    wq_r1 = uq[..., MLA_NOPE:MLA_NOPE + half].reshape(L, MLA_Q_LORA, 128)
    wq_r2 = uq[..., MLA_NOPE + half:].reshape(L, MLA_Q_LORA, 128)
    uk = jnp.transpose(mla_w_uk, (0, 2, 3, 1))
    z = jnp.zeros_like(uk[:, 0])
    wuk = jnp.stack([jnp.concatenate([jnp.concatenate([uk[:, 2 * p], z], axis=-1),
                                      jnp.concatenate([z, uk[:, 2 * p + 1]], axis=-1)], axis=1)
                     for p in range(MLA_HEADS // 2)], axis=1)
    uv = jnp.transpose(mla_w_uv, (0, 2, 1, 3))
    zv = jnp.zeros_like(uv[:, 0])
    wuv = jnp.concatenate(
        [jnp.concatenate([zv] * h + [uv[:, h]] + [zv] * (MLA_HEADS - 1 - h), axis=-1)
         for h in range(MLA_HEADS)], axis=1)
    mla = dict(gq=row(mla_q_norm_g), wq_n=wq_n.astype(BF16), wq_r1=wq_r1.astype(BF16),
               wq_r2=wq_r2.astype(BF16), gkv=row(mla_kv_norm_g), wuk=wuk.astype(BF16),
               wuv=wuv.astype(BF16))
    bf = jnp.concatenate([fox_b_f, jnp.zeros((L, 128 - FOX_HEADS), fox_b_f.dtype)],
                         axis=-1).reshape(L, 1, 128)
    dense = dict(w_in=w_in_p, wr=w_branch_rwkv.astype(BF16), wm=w_branch_mla.astype(BF16),
                 wf=w_branch_fox.astype(BF16), wo=w_out.astype(BF16), wu=w_ffn_up.astype(BF16),
                 wd=w_ffn_down.astype(BF16), bf=bf)
    return rwkv, mla, dense


def _rope_tables(pos):
    half = MLA_ROPE // 2
    inv = ROPE_THETA ** (-jnp.arange(half, dtype=F32) / half)
    ang = pos.astype(F32)[:, None] * inv[None, :]
    return jnp.tile(jnp.cos(ang), (1, 8)), jnp.tile(jnp.sin(ang), (1, 8))


def kernel(x_prompt, x_sample, cache_mla, cache_fox_k, cache_fox_v, cache_fox_logf, state_rwkv,
           state_rwkv_shift, page_table, norm_mix_g, w_in, rwkv_mu, rwkv_w0, rwkv_w2, rwkv_a0,
           rwkv_a2, rwkv_g2, rwkv_k_k, rwkv_k_a, rwkv_r_k, rwkv_ln_w, rwkv_ln_b, mla_q_norm_g,
           mla_w_uq, mla_kv_norm_g, mla_w_uk, mla_w_uv, fox_b_f, w_branch_rwkv, w_branch_mla,
           w_branch_fox, w_out, norm_ffn_g, w_ffn_up, w_ffn_down, norm_final_g):
    B, T, D = x_prompt.shape
    N = x_sample.shape[0]
    assert x_sample.shape[1] == 1
    n_pages = page_table.shape[1]
    past_len = n_pages * cache_mla.shape[2]
    rwkv_w, mla_w, dense_w = _prep_weights(
        w_in, rwkv_mu, rwkv_w0, rwkv_w2, rwkv_a0, rwkv_a2, rwkv_g2, rwkv_k_k, rwkv_k_a, rwkv_r_k,
        rwkv_ln_w, rwkv_ln_b, mla_q_norm_g, mla_w_uq, mla_kv_norm_g, mla_w_uk, mla_w_uv, fox_b_f,
        w_branch_rwkv, w_branch_mla, w_branch_fox, w_out, w_ffn_up, w_ffn_down)
    cos_p, sin_p = _rope_tables(jnp.arange(T))
    cos_s, sin_s = _rope_tables(jnp.full((N,), past_len))
    cache_fox_logf_t = jnp.swapaxes(cache_fox_logf, 2, 3)
    cache_fk = cache_fox_k.reshape(cache_fox_k.shape[:3] + (128,))
    cache_fv = cache_fox_v.reshape(cache_fox_v.shape[:3] + (128,))

    xp = x_prompt.reshape(B * T, D)
    xs = x_sample.reshape(N, D)
    prev_zero = jnp.zeros((B, 1, RWKV_COLS), F32)
    outs_p, outs_s = [], []
    for l in range(DEPTH):
        rw = {k: (v if k == "ones" else v[l]) for k, v in rwkv_w.items()}
        mw = {k: v[l] for k, v in mla_w.items()}
        dw = {k: v[l] for k, v in dense_w.items()}
        g_mix = norm_mix_g[l].reshape(1, D)
        g_ffn = norm_ffn_g[l].reshape(1, D)
        rparams = (rw["mu"], rw["w0"], rw["w2"], rw["a0"], rw["a2"], rw["g2"], rw["kk"], rw["ka"],
                   rw["ones"])

        z_r, c_q, c_kv, kr1, kr2, fq, fk, fv, fvs, ff, gates = _in_proj(xp, g_mix, dw["w_in"])
        seq = lambda a: a.reshape(B, T, a.shape[-1])
        r, lw, k2, v, a_s, b_s, g = _rwkv_prep_seq(seq(z_r), prev_zero, rparams)
        y, S_p = _wkv_seq(r, lw, k2, v, a_s, b_s)
        flat = lambda a: a.reshape(B * T, a.shape[-1])
        y_r = _rwkv_post(flat(y), flat(r), flat(k2), flat(v), flat(g), rw["lnw"], rw["lnb"],
                         rw["rk"], rw["ones"])
        qlat, qr, rows_p, ckv, krt = _mla_prep(seq(c_q), seq(c_kv), seq(kr1), seq(kr2), cos_p,
                                               sin_p, mw)
        y_m = _mla_attn(qlat, qr, ckv, krt, mw["wuv"])
        lf_p, c_cum, c_cum_t = _fox_prep(seq(ff), dw["bf"])
        y_f = _fox_attn(seq(fq), seq(fk), seq(fv), seq(fvs), c_cum, c_cum_t)
        xp = _merge(xp, y_r, flat(y_m), flat(y_f), gates, dw["wr"], dw["wm"], dw["wf"], dw["wo"])
        xp = _ffn(xp, g_ffn, dw["wu"], dw["wd"])
        outs_p.append((rows_p, fk.reshape(B, T, 2, 64), fv.reshape(B, T, 2, 64), lf_p, S_p,
                       seq(z_r)[:, -1]))

        z_r, c_q, c_kv, kr1, kr2, fq, fk, fv, fvs, ff, gates = _in_proj(xs, g_mix, dw["w_in"])
        r, lw, k2, v, a_s, b_s, g = _rwkv_prep_tok(z_r, state_rwkv_shift[l], rparams)
        hd = lambda a: a.reshape(N, RWKV_HEADS, 1, 64)
        y, S_s = _wkv_step(state_rwkv[l], hd(r), hd(lw), hd(k2), hd(v), hd(a_s), hd(b_s))
        y_r = _rwkv_post(y.reshape(N, 512), r, k2, v, g, rw["lnw"], rw["lnb"], rw["rk"],
                         rw["ones"])
        one = lambda a: a.reshape(1, N, a.shape[-1])
        qlat, qr, rows_s, _, _ = _mla_prep(one(c_q), one(c_kv), one(kr1), one(kr2), cos_s, sin_s,
                                           mw)
        qr_h = qr.reshape(N, 2, MLA_HEADS, 16)
        q288 = jnp.concatenate([qlat.reshape(N, MLA_HEADS, 256), qr_h[:, 0], qr_h[:, 1]], axis=-1)
        o_m = _mla_decode(page_table, q288, rows_s.reshape(N, 1, MLA_ROW), cache_mla, l)
        y_m = _matmul(o_m.reshape(N, MLA_HEADS * 256), mw["wuv"], BF16)
        three = lambda a: a.reshape(N, 1, a.shape[-1])
        o_f, lf_s = _fox_decode(page_table, fq.astype(BF16).reshape(N, FOX_HEADS, 128), three(fk),
                                three(fv), three(ff), dw["bf"], cache_fk, cache_fv,
                                cache_fox_logf_t, l)
        xs = _merge(xs, y_r, y_m, o_f.reshape(N, 512), gates, dw["wr"], dw["wm"], dw["wf"],
                    dw["wo"])
        xs = _ffn(xs, g_ffn, dw["wu"], dw["wd"])
        outs_s.append((rows_s.reshape(N, 1, MLA_ROW), fk.reshape(N, 1, 2, 64),
                       fv.reshape(N, 1, 2, 64), lf_s, S_s, z_r))

    g_fin = norm_final_g.reshape(1, D)
    y_prompt = _final_norm(xp, g_fin).reshape(B, T, D)
    y_sample = _final_norm(xs, g_fin).reshape(N, 1, D)
    stack = lambda outs: tuple(jnp.stack(t) for t in zip(*outs))
    return (y_prompt, y_sample) + stack(outs_p) + stack(outs_s)
```

```python
import functools

import jax
import jax.numpy as jnp
from jax import lax
from jax.experimental import pallas as pl
from jax.experimental.pallas import tpu as pltpu

F32 = jnp.float32
BF16 = jnp.bfloat16

D_MODEL = 1024
DEPTH = 4
PAGE_SIZE = 128
RWKV_HEADS = 8
RWKV_HEAD_DIM = 64
RWKV_DIM = 512
RWKV_COLS = 1792
GN_EPS = 64e-5
WKV_CHUNK = 64
WKV_GROUP = 4
MLA_HEADS = 8
MLA_Q_LORA = 384
MLA_KV_LORA = 256
MLA_NOPE = 64
MLA_ROPE = 32
MLA_V_DIM = 64
MLA_DIM = 512
MLA_ROW = 288
MLA_SCALE = (MLA_NOPE + MLA_ROPE) ** -0.5
ROPE_THETA = 10000.0
FOX_HEADS = 8
FOX_KV_HEADS = 2
FOX_GROUP = 4
FOX_HEAD_DIM = 64
FOX_DIM = 512
FOX_SCALE = FOX_HEAD_DIM ** -0.5
D_FF = 4096
RMS_EPS = 1e-6
NEG_INF = -1e30

IN_SEGS = (
    ("z_r", 1792), ("c_q", 384), ("c_kv", 256), ("kr1", 128), ("kr2", 128),
    ("fq", 1024), ("fk", 128), ("fv", 128), ("fvs", 128), ("ff", 128), ("gates", 3072),
)
IN_COLS_P = sum(w for _, w in IN_SEGS)

VMEM_LIMIT_BYTES = 56 * 1024 * 1024


def _cparams(sem):
    return pltpu.CompilerParams(dimension_semantics=sem, vmem_limit_bytes=VMEM_LIMIT_BYTES)


def _const_spec(shape):
    n = len(shape)
    return pl.BlockSpec(shape, lambda *_: (0,) * n)


def _dot(a, b):
    return jnp.dot(a, b, preferred_element_type=F32)


def _dot_nt(a, b):
    return lax.dot_general(a, b, (((1,), (1,)), ((), ())), preferred_element_type=F32)


def _dot_tn(a, b):
    return lax.dot_general(a, b, (((0,), (0,)), ((), ())), preferred_element_type=F32)


def _dot_hi(a, b):
    return jnp.dot(a, b, preferred_element_type=F32, precision=lax.Precision.HIGHEST)


def _split_dot(x, m):
    hi = x.astype(BF16)
    lo = (x - hi.astype(F32)).astype(BF16)
    return _dot(hi, m) + _dot(lo, m)


def _rms(x, g):
    return x * lax.rsqrt(jnp.mean(x * x, axis=-1, keepdims=True) + RMS_EPS) * g


def _sigmoid(x):
    return 1.0 / (1.0 + jnp.exp(-x))


def _log_sigmoid(x):
    return -(jnp.maximum(-x, 0.0) + jnp.log(1.0 + jnp.exp(-jnp.abs(x))))


def _softplus(x):
    return jnp.maximum(x, 0.0) + jnp.log(1.0 + jnp.exp(-jnp.abs(x)))


def _in_proj_kernel(x_ref, g_ref, w_ref, *out_refs):
    ub = _rms(x_ref[...], g_ref[...]).astype(BF16)
    off = 0
    for o_ref, (_, width) in zip(out_refs, IN_SEGS):
        o_ref[...] = _dot(ub, w_ref[:, off:off + width]).astype(o_ref.dtype)
        off += width


def _in_proj(x, g, w):
    m = x.shape[0]
    tm = min(256, m)
    outs = [jax.ShapeDtypeStruct((m, wd), F32) for _, wd in IN_SEGS]
    return pl.pallas_call(
        _in_proj_kernel,
        grid=(m // tm,),
        in_specs=[pl.BlockSpec((tm, D_MODEL), lambda i: (i, 0)),
                  _const_spec((1, D_MODEL)),
                  _const_spec((D_MODEL, IN_COLS_P))],
        out_specs=[pl.BlockSpec((tm, wd), lambda i: (i, 0)) for _, wd in IN_SEGS],
        out_shape=outs,
        compiler_params=_cparams(("parallel",)),
        name="in_proj",
    )(x, g, w)


def _rwkv_prep_math(z, zprev, mu, w0, w2p, a0, a2p, g2, kk_w, ka_w, ones_bd):
    zs = z + (zprev - z) * mu
    r = zs[:, 0:512]
    k = zs[:, 512:1024]
    v = zs[:, 1024:1536]
    zwa = zs[:, 1536:1664]
    zg = zs[:, 1664:1792]
    w = -_softplus(-(w0 + _dot(jnp.tanh(zwa).astype(BF16), w2p))) - 0.5
    a = _sigmoid(a0 + _dot(zwa.astype(BF16), a2p))
    g = _dot(_sigmoid(zg).astype(BF16), g2)
    kk = k * kk_w
    ss = _split_dot(kk * kk, ones_bd)
    kkn = kk * lax.rsqrt(jnp.maximum(ss, 1e-24))
    k2 = k * (1.0 + (a - 1.0) * ka_w)
    lw = -jnp.exp(w)
    return r, lw, k2, v, -kkn, kkn * a, g


def _rwkv_prep_seq_kernel(z_ref, prev_ref, mu_ref, w0_ref, w2_ref, a0_ref, a2_ref, g2_ref,
                          kk_ref, ka_ref, ones_ref,
                          r_ref, lw_ref, k_ref, v_ref, a_ref, b_ref, g_ref, carry):
    t = pl.program_id(1)

    @pl.when(t == 0)
    def _():
        carry[...] = jnp.broadcast_to(prev_ref[0], carry.shape)

    z = z_ref[0]
    rolled = pltpu.roll(z, 1, axis=0)
    row = lax.broadcasted_iota(jnp.int32, z.shape, 0)
    zprev = jnp.where(row == 0, carry[0:1, :], rolled)
    carry[...] = jnp.broadcast_to(z[z.shape[0] - 1:, :], carry.shape)
    outs = _rwkv_prep_math(z, zprev, mu_ref[...], w0_ref[...], w2_ref[...], a0_ref[...],
                           a2_ref[...], g2_ref[...], kk_ref[...], ka_ref[...], ones_ref[...])
    for o_ref, val in zip((r_ref, lw_ref, k_ref, v_ref, a_ref, b_ref, g_ref), outs):
        o_ref[0] = val


def _rwkv_prep_tok_kernel(z_ref, zprev_ref, mu_ref, w0_ref, w2_ref, a0_ref, a2_ref, g2_ref,
                          kk_ref, ka_ref, ones_ref,
                          r_ref, lw_ref, k_ref, v_ref, a_ref, b_ref, g_ref):
    outs = _rwkv_prep_math(z_ref[...], zprev_ref[...], mu_ref[...], w0_ref[...], w2_ref[...],
                           a0_ref[...], a2_ref[...], g2_ref[...], kk_ref[...], ka_ref[...],
                           ones_ref[...])
    for o_ref, val in zip((r_ref, lw_ref, k_ref, v_ref, a_ref, b_ref, g_ref), outs):
        o_ref[...] = val


def _rwkv_param_specs():
    return [_const_spec((1, RWKV_COLS)), _const_spec((1, 512)), _const_spec((128, 512)),
            _const_spec((1, 512)), _const_spec((128, 512)), _const_spec((128, 512)),
            _const_spec((1, 512)), _const_spec((1, 512)), _const_spec((512, 512))]


def _rwkv_prep_seq(z, prev, params):
    b, t, _ = z.shape
    tt = min(256, t)
    outs = [jax.ShapeDtypeStruct((b, t, 512), F32)] * 7
    return pl.pallas_call(
        _rwkv_prep_seq_kernel,
        grid=(b, t // tt),
        in_specs=[pl.BlockSpec((1, tt, RWKV_COLS), lambda i, j: (i, j, 0)),
                  pl.BlockSpec((1, 1, RWKV_COLS), lambda i, j: (i, 0, 0))] + _rwkv_param_specs(),
        out_specs=[pl.BlockSpec((1, tt, 512), lambda i, j: (i, j, 0))] * 7,
        out_shape=outs,
        scratch_shapes=[pltpu.VMEM((8, RWKV_COLS), F32)],
        compiler_params=_cparams(("parallel", "arbitrary")),
        name="rwkv_prep_seq",
    )(z, prev, *params)


def _rwkv_prep_tok(z, zprev, params):
    n = z.shape[0]
    outs = [jax.ShapeDtypeStruct((n, 512), F32)] * 7
    return pl.pallas_call(
        _rwkv_prep_tok_kernel,
        grid=(1,),
        in_specs=[_const_spec((n, RWKV_COLS)), _const_spec((n, RWKV_COLS))] + _rwkv_param_specs(),
        out_specs=[_const_spec((n, 512))] * 7,
        out_shape=outs,
        compiler_params=_cparams(("arbitrary",)),
        name="rwkv_prep_tok",
    )(z, zprev, *params)


def _block_diag(y, bd_mask):
    yb = y.astype(BF16)
    return jnp.where(bd_mask, jnp.concatenate([yb] * WKV_GROUP, axis=0), jnp.zeros((), BF16))


def _wkv_seq_kernel(r_ref, lw_ref, k_ref, v_ref, a_ref, b_ref, y_ref, s_ref, st):
    c = pl.program_id(1)
    C = WKV_CHUNK
    W = WKV_GROUP * RWKV_HEAD_DIM

    @pl.when(c == 0)
    def _():
        st[...] = jnp.zeros_like(st)

    ri = lax.broadcasted_iota(jnp.int32, (W, W), 0)
    ci = lax.broadcasted_iota(jnp.int32, (W, W), 1)
    bd_mask = (ri // RWKV_HEAD_DIM) == (ci // RWKV_HEAD_DIM)
    tr = lax.broadcasted_iota(jnp.int32, (C, W), 0)
    tc = lax.broadcasted_iota(jnp.int32, (C, W), 1) % C
    strict = tr > tc
    incl = tr >= tc
    eye = (tr == tc).astype(F32)
    cr = lax.broadcasted_iota(jnp.int32, (C, C), 0)
    cc = lax.broadcasted_iota(jnp.int32, (C, C), 1)
    tril = (cr >= cc).astype(F32)

    chains = [(n, slice(g * W, (g + 1) * W), g) for n in range(r_ref.shape[0])
              for g in range(RWKV_HEADS // WKV_GROUP)]
    each = lambda f, *cols: [f(*xs) for xs in zip(*cols)]
    bd = lambda y: _block_diag(y, bd_mask)
    b16 = lambda y: y.astype(BF16)
    load = lambda ref: [ref[n, :, sl] for n, sl, _ in chains]
    r, lw, k, v, a, b = (load(ref) for ref in (r_ref, lw_ref, k_ref, v_ref, a_ref, b_ref))
    hT = [st[n, g] for n, _, g in chains]
    L = each(lambda x: _dot_hi(tril, x), lw)
    e_in = each(jnp.exp, L)
    e_neg = each(lambda x: jnp.exp(-x), L)
    at = each(lambda x, y, z: x * jnp.exp(y - z), a, L, lw)
    rt = each(jnp.multiply, r, e_in)
    bt = each(jnp.multiply, b, e_neg)
    kt = each(jnp.multiply, k, e_neg)
    lhs = each(lambda x, y: b16(jnp.concatenate([x, y], axis=0)), at, rt)
    gb = each(lambda x, y: _dot_nt(x, bd(y)), lhs, bt)
    gk = each(lambda x, y: _dot_nt(x, bd(y)), lhs, kt)
    a_ab = each(lambda x: jnp.where(strict, x[:C], 0.0), gb)
    a_ak = each(lambda x: jnp.where(strict, x[:C], 0.0), gk)
    a_rb = each(lambda x: jnp.where(incl, x[C:], 0.0), gb)
    a_rk = each(lambda x: jnp.where(incl, x[C:], 0.0), gk)
    inv = each(lambda x: eye + x, a_ab)
    ap = a_ab
    span = 1
    while span * 2 < C:
        ap = each(lambda x: _dot(b16(x), bd(x)), ap)
        inv = each(lambda x, y: x + _dot(b16(x), bd(y)), inv, ap)
        span *= 2
    hTb = each(b16, hT)
    v_bd = each(bd, v)
    rhs = each(lambda x, h, y, z: _dot_nt(b16(x), h) + _dot(b16(y), z), at, hTb, a_ak, v_bd)
    u = each(lambda x, y: _dot(b16(x), bd(y)), inv, rhs)
    y = each(lambda x, h, p, q, s, z: _dot_nt(b16(x), h) + _dot(b16(p), bd(q)) + _dot(b16(s), z),
             rt, hTb, a_rb, u, a_rk, v_bd)
    upd = each(lambda p, q, s, z: _dot_tn(b16(p), b16(q)) + _dot_tn(b16(s), b16(z)), u, bt, v, kt)
    for (n, sl, g), yi, hi, ui, ei in zip(chains, y, hT, upd, e_in):
        y_ref[n, :, sl] = yi
        st[n, g] = jnp.where(bd_mask, (hi + ui) * ei[C - 1:C, :], 0.0)

    @pl.when(c == pl.num_programs(1) - 1)
    def _():
        for n in range(r_ref.shape[0]):
            for h in range(RWKV_HEADS):
                g, j = divmod(h, WKV_GROUP)
                lo = j * RWKV_HEAD_DIM
                s_ref[n, h] = st[n, g, lo:lo + RWKV_HEAD_DIM, lo:lo + RWKV_HEAD_DIM]


WKV_SEQS = 4


def _wkv_seq(r, lw, k, v, a, b):
    bsz, t, _ = r.shape
    nb = WKV_SEQS
    spec = pl.BlockSpec((nb, WKV_CHUNK, 512), lambda i, j: (i, j, 0))
    w = WKV_GROUP * RWKV_HEAD_DIM
    return pl.pallas_call(
        _wkv_seq_kernel,
        grid=(bsz // nb, t // WKV_CHUNK),
        in_specs=[spec] * 6,
        out_specs=[spec, pl.BlockSpec((nb, RWKV_HEADS, 64, 64), lambda i, j: (i, 0, 0, 0))],
        out_shape=[jax.ShapeDtypeStruct((bsz, t, 512), F32),
                   jax.ShapeDtypeStruct((bsz, RWKV_HEADS, 64, 64), F32)],
        scratch_shapes=[pltpu.VMEM((nb, RWKV_HEADS // WKV_GROUP, w, w), F32)],
        compiler_params=_cparams(("parallel", "arbitrary")),
        name="wkv_seq",
    )(r, lw, k, v, a, b)


def _wkv_step_kernel(s_ref, r_ref, lw_ref, k_ref, v_ref, a_ref, b_ref, y_ref, so_ref):
    s = s_ref[...]
    eye = (lax.broadcasted_iota(jnp.int32, (64, 64), 0)
           == lax.broadcasted_iota(jnp.int32, (64, 64), 1))[None, None]
    row = lambda ref: ref[...]
    w = jnp.exp(row(lw_ref))
    vcol = jnp.sum(jnp.where(eye, row(v_ref), 0.0), axis=-1, keepdims=True)
    sa = jnp.sum(s * row(a_ref), axis=-1, keepdims=True)
    s = s * w + sa * row(b_ref) + vcol * row(k_ref)
    ycol = jnp.sum(s * row(r_ref), axis=-1, keepdims=True)
    y_ref[...] = jnp.sum(jnp.where(eye, ycol, 0.0), axis=-2, keepdims=True)
    so_ref[...] = s


def _wkv_step(s, r, lw, k, v, a, b):
    n = s.shape[0]
    nb = 8
    vspec = pl.BlockSpec((nb, RWKV_HEADS, 1, 64), lambda i: (i, 0, 0, 0))
    sspec = pl.BlockSpec((nb, RWKV_HEADS, 64, 64), lambda i: (i, 0, 0, 0))
    return pl.pallas_call(
        _wkv_step_kernel,
        grid=(n // nb,),
        in_specs=[sspec] + [vspec] * 6,
        out_specs=[vspec, sspec],
        out_shape=[jax.ShapeDtypeStruct((n, RWKV_HEADS, 1, 64), F32),
                   jax.ShapeDtypeStruct(s.shape, F32)],
        compiler_params=_cparams(("parallel",)),
        name="wkv_step",
    )(s, r, lw, k, v, a, b)


def _rwkv_post_kernel(y_ref, r_ref, k_ref, v_ref, g_ref, lnw_ref, lnb_ref, rk_ref, ones_ref,
                      o_ref):
    y = y_ref[...]
    ones = ones_ref[...]
    inv_n = 1.0 / RWKV_HEAD_DIM
    mu = _split_dot(y, ones) * inv_n
    d = y - mu
    var = _split_dot(d * d, ones) * inv_n
    yn = d * lax.rsqrt(var + GN_EPS) * lnw_ref[...] + lnb_ref[...]
    bonus = _split_dot(r_ref[...] * k_ref[...] * rk_ref[...], ones) * v_ref[...]
    o_ref[...] = ((yn + bonus) * g_ref[...]).astype(o_ref.dtype)


def _rwkv_post(y, r, k, v, g, lnw, lnb, rk, ones_bd):
    m = y.shape[0]
    tm = min(512, m)
    spec = pl.BlockSpec((tm, 512), lambda i: (i, 0))
    return pl.pallas_call(
        _rwkv_post_kernel,
        grid=(m // tm,),
        in_specs=[spec] * 5 + [_const_spec((1, 512))] * 3 + [_const_spec((512, 512))],
        out_specs=spec,
        out_shape=jax.ShapeDtypeStruct((m, 512), BF16),
        compiler_params=_cparams(("parallel",)),
        name="rwkv_post",
    )(y, r, k, v, g, lnw, lnb, rk, ones_bd)


def _rope_lane_mask(h, shape):
    lane = lax.broadcasted_iota(jnp.int32, shape, len(shape) - 1) % 128
    return (lane >= 16 * h) & (lane < 16 * (h + 1))


def _mla_prep_kernel(zq_ref, zc_ref, zk1_ref, zk2_ref, cos_ref, sin_ref, gq_ref, wn_ref, wr1_ref,
                     wr2_ref, gkv_ref, wuk_ref, q_ref, rows_ref, k_ref):
    cos, sin = cos_ref[...], sin_ref[...]
    cq = _rms(zq_ref[0], gq_ref[...]).astype(BF16)
    qn = _dot(cq, wn_ref[...]) * MLA_SCALE
    q1 = _dot(cq, wr1_ref[...]) * MLA_SCALE
    q2 = _dot(cq, wr2_ref[...]) * MLA_SCALE
    qr = jnp.concatenate([q1 * cos - q2 * sin, q1 * sin + q2 * cos], axis=-1).astype(BF16)
    for p in range(MLA_HEADS // 2):
        ql = _dot(qn[:, 128 * p:128 * (p + 1)].astype(BF16), wuk_ref[p]).astype(BF16)
        for j in range(2):
            h = 2 * p + j
            q_ref[0, h, :, 0:256] = ql[:, 256 * j:256 * (j + 1)]
            q_ref[0, h, :, 256:512] = jnp.where(_rope_lane_mask(h, qr.shape), qr,
                                                jnp.zeros((), BF16))
    ckv = _rms(zc_ref[0], gkv_ref[...])
    z1, z2 = zk1_ref[0], zk2_ref[0]
    k1 = z1 * cos - z2 * sin
    k2 = z1 * sin + z2 * cos
    rows_ref[0, :, 0:256] = ckv
    rows_ref[0, :, 256:272] = k1[:, 0:16]
    rows_ref[0, :, 272:288] = k2[:, 0:16]
    k_ref[0] = jnp.concatenate([ckv, k1, k2], axis=-1).astype(BF16)


def _mla_prep(zq, zc, zk1, zk2, cos8, sin8, p):
    b, t, _ = zq.shape
    tt = min(256, t)
    io = lambda wd: pl.BlockSpec((1, tt, wd), lambda i, j: (i, j, 0))
    tab = pl.BlockSpec((tt, 128), lambda i, j: (j, 0))
    return pl.pallas_call(
        _mla_prep_kernel,
        grid=(b, t // tt),
        in_specs=[io(384), io(256), io(128), io(128), tab, tab,
                  _const_spec((1, 384)), _const_spec((384, 512)), _const_spec((384, 128)),
                  _const_spec((384, 128)), _const_spec((1, 256)), _const_spec((4, 128, 512))],
        out_specs=[pl.BlockSpec((1, MLA_HEADS, tt, 512), lambda i, j: (i, 0, j, 0)),
                   io(MLA_ROW), io(512)],
        out_shape=[jax.ShapeDtypeStruct((b, MLA_HEADS, t, 512), BF16),
                   jax.ShapeDtypeStruct((b, t, MLA_ROW), F32),
                   jax.ShapeDtypeStruct((b, t, 512), BF16)],
        compiler_params=_cparams(("parallel", "parallel")),
        name="mla_prep",
    )(zq, zc, zk1, zk2, cos8, sin8, p["gq"], p["wq_n"], p["wq_r1"], p["wq_r2"], p["gkv"],
      p["wuk"])


def _flash_init(m_scr, l_scr, acc_scr):
    m_scr[...] = jnp.full_like(m_scr, NEG_INF)
    l_scr[...] = jnp.zeros_like(l_scr)
    acc_scr[...] = jnp.zeros_like(acc_scr)


def _flash_step(q, k, v, qi, kj, tq, m_scr, l_scr, acc_scr):
    tk = k.shape[0]
    rows = q.shape[0]
    s = _dot_nt(q, k)
    qpos = qi * tq + lax.broadcasted_iota(jnp.int32, (rows, tk), 0) % tq
    kpos = kj * tk + lax.broadcasted_iota(jnp.int32, (rows, tk), 1)
    s = jnp.where(kpos <= qpos, s, NEG_INF)
    m_prev = m_scr[...]
    m_new = jnp.maximum(m_prev, jnp.max(s, axis=-1, keepdims=True))
    alpha = jnp.exp(m_prev - m_new)
    p = jnp.exp(s - jnp.concatenate([m_new] * (tk // 128), axis=1))
    psum = p[:, 0:128]
    for j in range(1, tk // 128):
        psum = psum + p[:, 128 * j:128 * (j + 1)]
    l_scr[...] = alpha * l_scr[...] + psum
    acc_scr[...] = (jnp.concatenate([alpha] * (v.shape[1] // 128), axis=1) * acc_scr[...]
                    + _dot(p.astype(BF16), v))
    m_scr[...] = m_new


def _flash_result(l_scr, acc_scr):
    return acc_scr[...] / jnp.sum(l_scr[...], axis=-1, keepdims=True)


def _causal_tiles(n):
    pairs = [(q, k) for q in range(n) for k in range(q + 1)]
    return (jnp.array([p[0] for p in pairs], jnp.int32),
            jnp.array([p[1] for p in pairs], jnp.int32))


def _mla_attn_kernel(qt_ref, kt_ref, q_ref, k_ref, wuv_ref, o_ref, m_scr, l_scr, acc_scr):
    step = pl.program_id(1)
    qi, kj = qt_ref[step], kt_ref[step]
    tq = q_ref.shape[2]

    @pl.when(kj == 0)
    def _():
        _flash_init(m_scr, l_scr, acc_scr)

    k = k_ref[0]
    _flash_step(q_ref[0].reshape(MLA_HEADS * tq, 512), k, k[:, 0:MLA_KV_LORA], qi, kj, tq,
                m_scr, l_scr, acc_scr)

    @pl.when(kj == qi)
    def _():
        o = _flash_result(l_scr, acc_scr).astype(BF16)
        o = jnp.concatenate([o[h * tq:(h + 1) * tq] for h in range(MLA_HEADS)], axis=-1)
        o_ref[0] = _dot(o, wuv_ref[...]).astype(o_ref.dtype)


def _mla_attn(q, k, wuv):
    b, _, t, _ = q.shape
    tq = tk = min(256, t)
    n = t // tq
    rows = MLA_HEADS * tq
    qt, kt = _causal_tiles(n)
    grid_spec = pltpu.PrefetchScalarGridSpec(
        num_scalar_prefetch=2,
        grid=(b, qt.shape[0]),
        in_specs=[pl.BlockSpec((1, MLA_HEADS, tq, 512), lambda i, s, qt, kt: (i, 0, qt[s], 0)),
                  pl.BlockSpec((1, tk, 512), lambda i, s, qt, kt: (i, kt[s], 0)),
                  pl.BlockSpec((2048, 512), lambda i, s, qt, kt: (0, 0))],
        out_specs=pl.BlockSpec((1, tq, 512), lambda i, s, qt, kt: (i, qt[s], 0)),
        scratch_shapes=[pltpu.VMEM((rows, 128), F32), pltpu.VMEM((rows, 128), F32),
                        pltpu.VMEM((rows, MLA_KV_LORA), F32)],
    )
    return pl.pallas_call(
        _mla_attn_kernel,
        grid_spec=grid_spec,
        out_shape=jax.ShapeDtypeStruct((b, t, 512), BF16),
        compiler_params=_cparams(("parallel", "arbitrary")),
        name="mla_attn",
    )(qt, kt, q, k, wuv)


FOX_C_PARTS = 3
FOX_AUX_BASE = 8


def _fox_aux_constants():
    n = FOX_C_PARTS
    pk = jnp.zeros((n * 128, 128), F32)
    pq = jnp.zeros((n * 128, FOX_HEADS * 128), F32)
    kconst = jnp.zeros((1, 128), F32).at[0, 0:n].set(1.0)
    qconst = jnp.zeros((1, FOX_HEADS * 128), F32)
    for h in range(FOX_HEADS):
        for i in range(n):
            pk = pk.at[i * 128 + h, FOX_AUX_BASE + n * h + i].set(1.0)
            pq = pq.at[i * 128 + h, 128 * h + i].set(1.0)
            qconst = qconst.at[0, 128 * h + FOX_AUX_BASE + n * h + i].set(-1.0)
    return pk.astype(BF16), pq.astype(BF16), kconst, qconst


def _bf16_parts(x, n):
    parts = []
    for _ in range(n):
        p = x.astype(BF16)
        parts.append(p)
        x = x - p.astype(F32)
    return jnp.concatenate(parts, axis=1)


def _fox_prep_kernel(fq_ref, fk_ref, fv_ref, fvs_ref, ff_ref, bf_ref, pk_ref, pq_ref, kc_ref,
                     qc_ref, lf_ref, q_ref, k_ref, v_ref, carry):
    t = pl.program_id(1)
    tt = ff_ref.shape[1]

    @pl.when(t == 0)
    def _():
        carry[...] = jnp.zeros_like(carry)

    lane = lax.broadcasted_iota(jnp.int32, (tt, 128), 1)
    lf = jnp.where(lane < FOX_HEADS, _log_sigmoid(ff_ref[0] + bf_ref[...]), 0.0)
    lf_ref[0] = lf[:, 0:FOX_HEADS]
    ri = lax.broadcasted_iota(jnp.int32, (tt, tt), 0)
    ci = lax.broadcasted_iota(jnp.int32, (tt, tt), 1)
    c = _dot_hi((ri >= ci).astype(F32), lf) + carry[0:1, :]
    carry[...] = jnp.broadcast_to(c[tt - 1:tt, :], carry.shape)
    parts = _bf16_parts(c, FOX_C_PARTS)
    kaux = _dot(parts, pk_ref[...]) + kc_ref[...]
    qaux = _dot(parts, pq_ref[...]) + qc_ref[...]
    k_ref[0] = jnp.concatenate([fk_ref[0], kaux], axis=1).astype(BF16)
    v_ref[0] = jnp.concatenate([fv_ref[0], fvs_ref[0]], axis=1).astype(BF16)
    for h in range(FOX_HEADS):
        sl = slice(128 * h, 128 * (h + 1))
        q_ref[0, h] = jnp.concatenate([fq_ref[0, :, sl], qaux[:, sl]], axis=1).astype(BF16)


def _fox_prep(fq, fk, fv, fvs, ff, bf, aux):
    b, t, _ = ff.shape
    tt = min(256, t)
    io = lambda wd: pl.BlockSpec((1, tt, wd), lambda i, j: (i, j, 0))
    n = FOX_C_PARTS * 128
    return pl.pallas_call(
        _fox_prep_kernel,
        grid=(b, t // tt),
        in_specs=[io(1024), io(128), io(128), io(128), io(128), _const_spec((1, 128)),
                  _const_spec((n, 128)), _const_spec((n, 1024)), _const_spec((1, 128)),
                  _const_spec((1, 1024))],
        out_specs=[io(FOX_HEADS),
                   pl.BlockSpec((1, FOX_HEADS, tt, 256), lambda i, j: (i, 0, j, 0)),
                   io(256), io(256)],
        out_shape=[jax.ShapeDtypeStruct((b, t, FOX_HEADS), F32),
                   jax.ShapeDtypeStruct((b, FOX_HEADS, t, 256), BF16),
                   jax.ShapeDtypeStruct((b, t, 256), BF16),
                   jax.ShapeDtypeStruct((b, t, 256), BF16)],
        scratch_shapes=[pltpu.VMEM((8, 128), F32)],
        compiler_params=_cparams(("parallel", "arbitrary")),
        name="fox_prep",
    )(fq, fk, fv, fvs, ff, bf, *aux)


def _fox_attn_kernel(qt_ref, kt_ref, q_ref, k_ref, v_ref, o_ref, m_scr, l_scr, acc_scr):
    step = pl.program_id(1)
    qi, kj = qt_ref[step], kt_ref[step]
    tq = q_ref.shape[2]

    @pl.when(kj == 0)
    def _():
        _flash_init(m_scr, l_scr, acc_scr)

    _flash_step(q_ref[0].reshape(FOX_HEADS * tq, 256), k_ref[0], v_ref[0], qi, kj, tq,
                m_scr, l_scr, acc_scr)

    @pl.when(kj == qi)
    def _():
        o = _flash_result(l_scr, acc_scr)
        lane = lax.broadcasted_iota(jnp.int32, (tq, 128), 1)

        def head(h):
            plain = (h % 2 == 0) == (h // FOX_GROUP == 0)
            return o[h * tq:(h + 1) * tq, 0:128] if plain else o[h * tq:(h + 1) * tq, 128:256]

        for p2 in range(FOX_HEADS // 2):
            o_ref[0, :, 128 * p2:128 * (p2 + 1)] = jnp.where(
                lane < 64, head(2 * p2), head(2 * p2 + 1)).astype(o_ref.dtype)


def _fox_attn(q, k, v):
    b, _, t, _ = q.shape
    tq = tk = min(256, t)
    n = t // tq
    rows = FOX_HEADS * tq
    qt, kt = _causal_tiles(n)
    kspec = pl.BlockSpec((1, tk, 256), lambda i, s, qt, kt: (i, kt[s], 0))
    grid_spec = pltpu.PrefetchScalarGridSpec(
        num_scalar_prefetch=2,
        grid=(b, qt.shape[0]),
        in_specs=[pl.BlockSpec((1, FOX_HEADS, tq, 256), lambda i, s, qt, kt: (i, 0, qt[s], 0)),
                  kspec, kspec],
        out_specs=pl.BlockSpec((1, tq, 512), lambda i, s, qt, kt: (i, qt[s], 0)),
        scratch_shapes=[pltpu.VMEM((rows, 128), F32), pltpu.VMEM((rows, 128), F32),
                        pltpu.VMEM((rows, 256), F32)],
    )
    return pl.pallas_call(
        _fox_attn_kernel,
        grid_spec=grid_spec,
        out_shape=jax.ShapeDtypeStruct((b, t, 512), BF16),
        compiler_params=_cparams(("parallel", "arbitrary")),
        name="fox_attn",
    )(qt, kt, q, k, v)


DEC_PAGES = 32
DEC_BLOCK = 256


def _page_copies(pt_ref, seq, chunk, slot, layer, pairs, sem):
    out = []
    for j in range(DEC_PAGES):
        page = pt_ref[seq, chunk * DEC_PAGES + j]
        for cache_ref, buf_ref in pairs:
            out.append(pltpu.make_async_copy(cache_ref.at[layer, page], buf_ref.at[slot, j],
                                             sem.at[slot]))
    return out


def _paged_pipeline(pt_ref, pairs, sem, layer):
    n_chunks = pl.num_programs(1)
    step = pl.program_id(0) * n_chunks + pl.program_id(1)
    total = pl.num_programs(0) * n_chunks
    slot = step % 2

    @pl.when(step == 0)
    def _():
        for cp in _page_copies(pt_ref, 0, 0, 0, layer, pairs, sem):
            cp.start()

    nxt = step + 1

    @pl.when(nxt < total)
    def _():
        for cp in _page_copies(pt_ref, nxt // n_chunks, nxt % n_chunks, 1 - slot, layer, pairs, sem):
            cp.start()

    for cp in _page_copies(pt_ref, pl.program_id(0), pl.program_id(1), slot, layer, pairs, sem):
        cp.wait()
    return slot


def _mla_dec_kernel(pt_ref, q_ref, new_ref, cache_ref, o_ref, buf, sem, m_scr, l_scr, acc_scr, *,
                    layer):
    c = pl.program_id(1)
    slot = _paged_pipeline(pt_ref, [(cache_ref, buf)], sem, layer)

    @pl.when(c == 0)
    def _():
        m_scr[...] = jnp.full_like(m_scr, NEG_INF)
        l_scr[...] = jnp.zeros_like(l_scr)
        acc_scr[...] = jnp.zeros_like(acc_scr)

    q = q_ref[0]
    nblk = DEC_PAGES * PAGE_SIZE // DEC_BLOCK
    ppb = DEC_BLOCK // PAGE_SIZE
    kts = [jnp.concatenate([buf[slot, blk * ppb + j] for j in range(ppb)], axis=1).astype(BF16)
           for blk in range(nblk)]
    s = jnp.stack([_dot(q, kt) for kt in kts], axis=0)
    m = m_scr[...]
    m_new = jnp.maximum(m, jnp.max(jnp.max(s, axis=-1, keepdims=True), axis=0))
    alpha = jnp.exp(m - m_new)
    p = jnp.exp(s - m_new[None])
    l = alpha * l_scr[...] + jnp.sum(jnp.sum(p, axis=-1, keepdims=True), axis=0)
    acc = alpha * acc_scr[...]
    for blk in range(nblk):
        acc = acc + _dot_nt(p[blk].astype(BF16), kts[blk][0:MLA_KV_LORA, :])
    m = m_new
    m_scr[...], l_scr[...], acc_scr[...] = m, l, acc

    @pl.when(c == pl.num_programs(1) - 1)
    def _():
        new = new_ref[0].astype(BF16).astype(F32)
        s_new = jnp.sum(q.astype(F32) * new, axis=-1, keepdims=True)
        m_fin = jnp.maximum(m, s_new)
        w_old = jnp.exp(m - m_fin)
        p_new = jnp.exp(s_new - m_fin).astype(BF16).astype(F32)
        den = l * w_old + p_new
        o_ref[0] = ((acc * w_old + p_new * new[:, 0:MLA_KV_LORA]) / den).astype(o_ref.dtype)


def _mla_decode(page_table, q, new_rows, cache, layer):
    n, n_pages = page_table.shape
    grid_spec = pltpu.PrefetchScalarGridSpec(
        num_scalar_prefetch=1,
        grid=(n, n_pages // DEC_PAGES),
        in_specs=[pl.BlockSpec((1, MLA_HEADS, MLA_ROW), lambda i, c, pt: (i, 0, 0)),
                  pl.BlockSpec((1, 1, MLA_ROW), lambda i, c, pt: (i, 0, 0)),
                  pl.BlockSpec(memory_space=pl.ANY)],
        out_specs=pl.BlockSpec((1, MLA_HEADS, MLA_KV_LORA), lambda i, c, pt: (i, 0, 0)),
        scratch_shapes=[pltpu.VMEM((2, DEC_PAGES, MLA_ROW, PAGE_SIZE), F32),
                        pltpu.SemaphoreType.DMA((2,)),
                        pltpu.VMEM((MLA_HEADS, 1), F32), pltpu.VMEM((MLA_HEADS, 1), F32),
                        pltpu.VMEM((MLA_HEADS, MLA_KV_LORA), F32)],
    )
    return pl.pallas_call(
        functools.partial(_mla_dec_kernel, layer=layer),
        grid_spec=grid_spec,
        out_shape=jax.ShapeDtypeStruct((n, MLA_HEADS, MLA_KV_LORA), BF16),
        compiler_params=_cparams(("arbitrary", "arbitrary")),
        name="mla_decode",
    )(page_table, q, new_rows, cache)


def _fox_dec_kernel(pt_ref, q_ref, kn_ref, vn_ref, ffn_ref, bf_ref, kc_ref, vc_ref, lc_ref,
                    o_ref, lf_ref, kbuf, vbuf, lbuf, sem, m_scr, l_scr, acc_scr, run_scr, *, layer):
    c = pl.program_id(1)
    slot = _paged_pipeline(pt_ref, [(kc_ref, kbuf), (vc_ref, vbuf), (lc_ref, lbuf)], sem, layer)
    H = FOX_HEADS
    nblk = DEC_PAGES * PAGE_SIZE // DEC_BLOCK
    ppb = DEC_BLOCK // PAGE_SIZE
    R = nblk * H

    @pl.when(c == 0)
    def _():
        m_scr[...] = jnp.full_like(m_scr, NEG_INF)
        l_scr[...] = jnp.zeros_like(l_scr)
        acc_scr[...] = jnp.zeros_like(acc_scr)
        run_scr[...] = jnp.zeros_like(run_scr)

    q = q_ref[0]
    lf = jnp.concatenate(
        [jnp.concatenate([lbuf[slot, blk * ppb + j] for j in range(ppb)], axis=1)
         for blk in range(nblk)], axis=0)
    ri = lax.broadcasted_iota(jnp.int32, (DEC_BLOCK, DEC_BLOCK), 0)
    ci = lax.broadcasted_iota(jnp.int32, (DEC_BLOCK, DEC_BLOCK), 1)
    cum = _dot_hi(lf, (ri <= ci).astype(F32))
    rr = lax.broadcasted_iota(jnp.int32, (R, R), 0)
    rc = lax.broadcasted_iota(jnp.int32, (R, R), 1)
    same_head = (rr % H) == (rc % H)
    tot = cum[:, DEC_BLOCK - 1:DEC_BLOCK]
    tot_b = jnp.broadcast_to(tot, (R, 128))
    before = _dot_hi((same_head & (rc // H < rr // H)).astype(F32), tot_b)
    whole = _dot_hi(same_head.astype(F32), tot_b)
    run = run_scr[...]
    cpast = cum + before[:, 0:1] + run[:, 0:1]
    run_scr[...] = run + whole

    s = jnp.concatenate(
        [_dot(q, jnp.concatenate([kbuf[slot, blk * ppb + j] for j in range(ppb)],
                                 axis=1).astype(BF16)) for blk in range(nblk)], axis=0)
    x = (s - cpast).reshape(nblk, H, DEC_BLOCK)
    m = m_scr[...]
    m_new = jnp.maximum(m, jnp.max(jnp.max(x, axis=-1, keepdims=True), axis=0))
    alpha = jnp.exp(m - m_new)
    p = jnp.exp(x - m_new[None])
    l_scr[...] = alpha * l_scr[...] + jnp.sum(jnp.sum(p, axis=-1, keepdims=True), axis=0)
    acc = alpha * acc_scr[...]
    for blk in range(nblk):
        vt = jnp.concatenate([vbuf[slot, blk * ppb + j] for j in range(ppb)], axis=1).astype(BF16)
        acc = acc + _dot_nt(p[blk].astype(BF16), vt)
    acc_scr[...] = acc
    m_scr[...] = m_new

    @pl.when(c == pl.num_programs(1) - 1)
    def _():
        diag = (lax.broadcasted_iota(jnp.int32, (H, 128), 0)
                == lax.broadcasted_iota(jnp.int32, (H, 128), 1))
        lf_new_row = _log_sigmoid(ffn_ref[0] + bf_ref[...])
        lf_ref[0] = lf_new_row[:, 0:H]
        lf_new = jnp.sum(jnp.where(diag, lf_new_row, 0.0), axis=-1, keepdims=True)
        c_new = run_scr[0:H, 0:1] + lf_new
        kn = kn_ref[0].astype(BF16).astype(F32)
        vn = vn_ref[0].astype(BF16).astype(F32)
        s_new = jnp.sum(q.astype(F32) * kn, axis=-1, keepdims=True) + (c_new - c_new)
        m_past = m_scr[...] + c_new
        m_fin = jnp.maximum(m_past, s_new)
        w_old = jnp.exp(m_past - m_fin)
        p_new = jnp.exp(s_new - m_fin).astype(BF16).astype(F32)
        den = l_scr[...] * w_old + p_new
        o = (acc_scr[...] * w_old + p_new * vn) / den
        head = lax.broadcasted_iota(jnp.int32, (H, 64), 0)
        o_ref[0] = jnp.where(head < FOX_GROUP, o[:, 0:64], o[:, 64:128]).astype(o_ref.dtype)


def _fox_decode(page_table, q, k_new, v_new, ff_new, bf, cache_k, cache_v, cache_lt, layer):
    n, n_pages = page_table.shape
    row = lambda wd: pl.BlockSpec((1, 1, wd), lambda i, c, pt: (i, 0, 0))
    rows = DEC_PAGES * PAGE_SIZE // DEC_BLOCK * FOX_HEADS
    grid_spec = pltpu.PrefetchScalarGridSpec(
        num_scalar_prefetch=1,
        grid=(n, n_pages // DEC_PAGES),
        in_specs=[pl.BlockSpec((1, FOX_HEADS, 128), lambda i, c, pt: (i, 0, 0)),
                  row(128), row(128), row(128),
                  pl.BlockSpec((1, 128), lambda i, c, pt: (0, 0)),
                  pl.BlockSpec(memory_space=pl.ANY), pl.BlockSpec(memory_space=pl.ANY),
                  pl.BlockSpec(memory_space=pl.ANY)],
        out_specs=[pl.BlockSpec((1, FOX_HEADS, 64), lambda i, c, pt: (i, 0, 0)),
                   pl.BlockSpec((1, 1, FOX_HEADS), lambda i, c, pt: (i, 0, 0))],
        scratch_shapes=[pltpu.VMEM((2, DEC_PAGES, PAGE_SIZE, 128), F32),
                        pltpu.VMEM((2, DEC_PAGES, PAGE_SIZE, 128), F32),
                        pltpu.VMEM((2, DEC_PAGES, FOX_HEADS, PAGE_SIZE), F32),
                        pltpu.SemaphoreType.DMA((2,)),
                        pltpu.VMEM((FOX_HEADS, 1), F32), pltpu.VMEM((FOX_HEADS, 1), F32),
                        pltpu.VMEM((FOX_HEADS, 128), F32), pltpu.VMEM((rows, 128), F32)],
    )
    return pl.pallas_call(
        functools.partial(_fox_dec_kernel, layer=layer),
        grid_spec=grid_spec,
        out_shape=[jax.ShapeDtypeStruct((n, FOX_HEADS, 64), BF16),
                   jax.ShapeDtypeStruct((n, 1, FOX_HEADS), F32)],
        compiler_params=_cparams(("arbitrary", "arbitrary")),
        name="fox_decode",
    )(page_table, q, k_new, v_new, ff_new, bf, cache_k, cache_v, cache_lt)


def _matmul_kernel(x_ref, w_ref, o_ref):
    o_ref[...] = _dot(x_ref[...], w_ref[...]).astype(o_ref.dtype)


def _matmul(x, w, out_dtype):
    m, k = x.shape
    n = w.shape[1]
    tm = min(256, m)
    return pl.pallas_call(
        _matmul_kernel,
        grid=(m // tm,),
        in_specs=[pl.BlockSpec((tm, k), lambda i: (i, 0)), _const_spec((k, n))],
        out_specs=pl.BlockSpec((tm, n), lambda i: (i, 0)),
        out_shape=jax.ShapeDtypeStruct((m, n), out_dtype),
        compiler_params=_cparams(("parallel",)),
        name="matmul",
    )(x, w)


def _merge_kernel(x_ref, yr_ref, ym_ref, yf_ref, g_ref, wr_ref, wm_ref, wf_ref, wo_ref, o_ref):
    merged = (_sigmoid(g_ref[:, 0:1024]) * _dot(yr_ref[...], wr_ref[...])
              + _sigmoid(g_ref[:, 1024:2048]) * _dot(ym_ref[...], wm_ref[...])
              + _sigmoid(g_ref[:, 2048:3072]) * _dot(yf_ref[...], wf_ref[...]))
    o_ref[...] = x_ref[...] + _dot(merged.astype(BF16), wo_ref[...])


def _merge(x, yr, ym, yf, gates, wr, wm, wf, wo):
    m = x.shape[0]
    tm = min(256, m)
    row = lambda wd: pl.BlockSpec((tm, wd), lambda i: (i, 0))
    return pl.pallas_call(
        _merge_kernel,
        grid=(m // tm,),
        in_specs=[row(D_MODEL), row(512), row(512), row(512), row(3072),
                  _const_spec((512, D_MODEL)), _const_spec((512, D_MODEL)),
                  _const_spec((512, D_MODEL)), _const_spec((D_MODEL, D_MODEL))],
        out_specs=row(D_MODEL),
        out_shape=jax.ShapeDtypeStruct((m, D_MODEL), F32),
        compiler_params=_cparams(("parallel",)),
        name="merge",
    )(x, yr, ym, yf, gates, wr, wm, wf, wo)


FFN_CHUNK = 1024


def _ffn_kernel(x_ref, g_ref, wu_ref, wd_ref, o_ref):
    x = x_ref[...]
    ub = _rms(x, g_ref[...]).astype(BF16)
    acc = x
    for c in range(D_FF // FFN_CHUNK):
        sl = slice(c * FFN_CHUNK, (c + 1) * FFN_CHUNK)
        h = jnp.maximum(_dot(ub, wu_ref[:, sl]), 0.0)
        acc = acc + _dot((h * h).astype(BF16), wd_ref[sl, :])
    o_ref[...] = acc


def _ffn(x, g, wu, wd):
    m = x.shape[0]
    tm = min(256, m)
    row = pl.BlockSpec((tm, D_MODEL), lambda i: (i, 0))
    return pl.pallas_call(
        _ffn_kernel,
        grid=(m // tm,),
        in_specs=[row, _const_spec((1, D_MODEL)), _const_spec((D_MODEL, D_FF)),
                  _const_spec((D_FF, D_MODEL))],
        out_specs=row,
        out_shape=jax.ShapeDtypeStruct((m, D_MODEL), F32),
        compiler_params=_cparams(("parallel",)),
        name="ffn",
    )(x, g, wu, wd)


def _final_norm_kernel(x_ref, g_ref, o_ref):
    o_ref[...] = _rms(x_ref[...], g_ref[...])


def _final_norm(x, g):
    m = x.shape[0]
    tm = min(512, m)
    row = pl.BlockSpec((tm, D_MODEL), lambda i: (i, 0))
    return pl.pallas_call(
        _final_norm_kernel,
        grid=(m // tm,),
        in_specs=[row, _const_spec((1, D_MODEL))],
        out_specs=row,
        out_shape=jax.ShapeDtypeStruct((m, D_MODEL), F32),
        compiler_params=_cparams(("parallel",)),
        name="final_norm",
    )(x, g)


def _prep_weights(w_in, rwkv_mu, rwkv_w0, rwkv_w2, rwkv_a0, rwkv_a2, rwkv_g2, rwkv_k_k, rwkv_k_a,
                  rwkv_r_k, rwkv_ln_w, rwkv_ln_b, mla_q_norm_g, mla_w_uq, mla_kv_norm_g, mla_w_uk,
                  mla_w_uv, fox_b_f, w_branch_rwkv, w_branch_mla, w_branch_fox, w_out, w_ffn_up,
                  w_ffn_down):
    L = w_in.shape[0]
    o_m = RWKV_COLS
    o_kr = o_m + MLA_Q_LORA + MLA_KV_LORA
    o_f = o_kr + MLA_ROPE
    o_fk = o_f + FOX_DIM
    o_fv = o_fk + 128
    o_ff = o_fv + 128
    o_g = o_ff + FOX_HEADS
    zeros = lambda n: jnp.zeros((L, D_MODEL, n), w_in.dtype)
    half = MLA_ROPE // 2
    fq = w_in[:, :, o_f:o_fk].reshape(L, D_MODEL, FOX_HEADS, 64) * FOX_SCALE
    fq_tiles = []
    for h in range(FOX_HEADS):
        parts = [fq[:, :, h], zeros(64)]
        fq_tiles.append(jnp.concatenate(parts if h // FOX_GROUP == 0 else parts[::-1], axis=-1))
    fv = w_in[:, :, o_fv:o_ff]
    w_in_p = jnp.concatenate(
        [w_in[:, :, 0:o_kr],
         jnp.tile(w_in[:, :, o_kr:o_kr + half], (1, 1, 8)),
         jnp.tile(w_in[:, :, o_kr + half:o_f], (1, 1, 8))]
        + fq_tiles
        + [w_in[:, :, o_fk:o_fv], fv, jnp.concatenate([fv[:, :, 64:], fv[:, :, :64]], axis=-1),
           w_in[:, :, o_ff:o_g], zeros(128 - FOX_HEADS), w_in[:, :, o_g:]], axis=-1).astype(BF16)
    assert w_in_p.shape[-1] == IN_COLS_P

    pad_rows = lambda w, top: jnp.concatenate(
        [jnp.zeros_like(w), w] if top else [w, jnp.zeros_like(w)], axis=1).astype(BF16)
    row = lambda v: v.reshape(L, 1, -1)
    idx = jnp.arange(512)
    ones_bd = ((idx[:, None] // 64) == (idx[None, :] // 64)).astype(BF16)
    rwkv = dict(mu=row(rwkv_mu), w0=row(rwkv_w0), w2=pad_rows(rwkv_w2, False), a0=row(rwkv_a0),
                a2=pad_rows(rwkv_a2, True), g2=rwkv_g2.astype(BF16), kk=row(rwkv_k_k),
                ka=row(rwkv_k_a), rk=row(rwkv_r_k), lnw=row(rwkv_ln_w), lnb=row(rwkv_ln_b),
                ones=ones_bd)

    uq = mla_w_uq.reshape(L, MLA_Q_LORA, MLA_HEADS, MLA_NOPE + MLA_ROPE)
    wq_n = uq[..., :MLA_NOPE].reshape(L, MLA_Q_LORA, 512)
    wq_r1 = uq[..., MLA_NOPE:MLA_NOPE + half].reshape(L, MLA_Q_LORA, 128)
    wq_r2 = uq[..., MLA_NOPE + half:].reshape(L, MLA_Q_LORA, 128)
    uk = jnp.transpose(mla_w_uk, (0, 2, 3, 1))
    z = jnp.zeros_like(uk[:, 0])
    wuk = jnp.stack([jnp.concatenate([jnp.concatenate([uk[:, 2 * p], z], axis=-1),
                                      jnp.concatenate([z, uk[:, 2 * p + 1]], axis=-1)], axis=1)
                     for p in range(MLA_HEADS // 2)], axis=1)
    uv = jnp.transpose(mla_w_uv, (0, 2, 1, 3))
    zv = jnp.zeros_like(uv[:, 0])
    wuv = jnp.concatenate(
        [jnp.concatenate([zv] * h + [uv[:, h]] + [zv] * (MLA_HEADS - 1 - h), axis=-1)
         for h in range(MLA_HEADS)], axis=1)
    mla = dict(gq=row(mla_q_norm_g), wq_n=wq_n.astype(BF16), wq_r1=wq_r1.astype(BF16),
               wq_r2=wq_r2.astype(BF16), gkv=row(mla_kv_norm_g), wuk=wuk.astype(BF16),
               wuv=wuv.astype(BF16))
    bf = jnp.concatenate([fox_b_f, jnp.zeros((L, 128 - FOX_HEADS), fox_b_f.dtype)],
                         axis=-1).reshape(L, 1, 128)
    dense = dict(w_in=w_in_p, wr=w_branch_rwkv.astype(BF16), wm=w_branch_mla.astype(BF16),
                 wf=w_branch_fox.astype(BF16), wo=w_out.astype(BF16), wu=w_ffn_up.astype(BF16),
                 wd=w_ffn_down.astype(BF16), bf=bf)
    return rwkv, mla, dense


def _rope_tables(pos):
    half = MLA_ROPE // 2
    inv = ROPE_THETA ** (-jnp.arange(half, dtype=F32) / half)
    ang = pos.astype(F32)[:, None] * inv[None, :]
    return jnp.tile(jnp.cos(ang), (1, 8)), jnp.tile(jnp.sin(ang), (1, 8))


def kernel(x_prompt, x_sample, cache_mla, cache_fox_k, cache_fox_v, cache_fox_logf, state_rwkv,
           state_rwkv_shift, page_table, norm_mix_g, w_in, rwkv_mu, rwkv_w0, rwkv_w2, rwkv_a0,
           rwkv_a2, rwkv_g2, rwkv_k_k, rwkv_k_a, rwkv_r_k, rwkv_ln_w, rwkv_ln_b, mla_q_norm_g,
           mla_w_uq, mla_kv_norm_g, mla_w_uk, mla_w_uv, fox_b_f, w_branch_rwkv, w_branch_mla,
           w_branch_fox, w_out, norm_ffn_g, w_ffn_up, w_ffn_down, norm_final_g):
    B, T, D = x_prompt.shape
    N = x_sample.shape[0]
    assert x_sample.shape[1] == 1
    n_pages = page_table.shape[1]
    past_len = n_pages * cache_mla.shape[2]
    rwkv_w, mla_w, dense_w = _prep_weights(
        w_in, rwkv_mu, rwkv_w0, rwkv_w2, rwkv_a0, rwkv_a2, rwkv_g2, rwkv_k_k, rwkv_k_a, rwkv_r_k,
        rwkv_ln_w, rwkv_ln_b, mla_q_norm_g, mla_w_uq, mla_kv_norm_g, mla_w_uk, mla_w_uv, fox_b_f,
        w_branch_rwkv, w_branch_mla, w_branch_fox, w_out, w_ffn_up, w_ffn_down)
    cos_p, sin_p = _rope_tables(jnp.arange(T))
    cos_s, sin_s = _rope_tables(jnp.full((N,), past_len))
    cache_mla_t = jnp.swapaxes(cache_mla, 2, 3)
    cache_fox_logf_t = jnp.swapaxes(cache_fox_logf, 2, 3)
    feat_major = lambda c: jnp.transpose(c, (0, 1, 3, 4, 2)).reshape(c.shape[:2] + (128, PAGE_SIZE))
    cache_fk = feat_major(cache_fox_k)
    cache_fv = feat_major(cache_fox_v)
    fox_aux = _fox_aux_constants()

    xp = x_prompt.reshape(B * T, D)
    xs = x_sample.reshape(N, D)
    prev_zero = jnp.zeros((B, 1, RWKV_COLS), F32)
    outs_p, outs_s = [], []
    for l in range(DEPTH):
        rw = {k: (v if k == "ones" else v[l]) for k, v in rwkv_w.items()}
        mw = {k: v[l] for k, v in mla_w.items()}
        dw = {k: v[l] for k, v in dense_w.items()}
        g_mix = norm_mix_g[l].reshape(1, D)
        g_ffn = norm_ffn_g[l].reshape(1, D)
        rparams = (rw["mu"], rw["w0"], rw["w2"], rw["a0"], rw["a2"], rw["g2"], rw["kk"], rw["ka"],
                   rw["ones"])

        z_r, c_q, c_kv, kr1, kr2, fq, fk, fv, fvs, ff, gates = _in_proj(xp, g_mix, dw["w_in"])
        seq = lambda a: a.reshape(B, T, a.shape[-1])
        r, lw, k2, v, a_s, b_s, g = _rwkv_prep_seq(seq(z_r), prev_zero, rparams)
        y, S_p = _wkv_seq(r, lw, k2, v, a_s, b_s)
        flat = lambda a: a.reshape(B * T, a.shape[-1])
        y_r = _rwkv_post(flat(y), flat(r), flat(k2), flat(v), flat(g), rw["lnw"], rw["lnb"],
                         rw["rk"], rw["ones"])
        q_m, rows_p, k_m = _mla_prep(seq(c_q), seq(c_kv), seq(kr1), seq(kr2), cos_p, sin_p, mw)
        y_m = _mla_attn(q_m, k_m, mw["wuv"])
        lf_p, q_f, k_f, v_f = _fox_prep(seq(fq), seq(fk), seq(fv), seq(fvs), seq(ff), dw["bf"],
                                        fox_aux)
        y_f = _fox_attn(q_f, k_f, v_f)
        xp = _merge(xp, y_r, flat(y_m), flat(y_f), gates, dw["wr"], dw["wm"], dw["wf"], dw["wo"])
        xp = _ffn(xp, g_ffn, dw["wu"], dw["wd"])
        outs_p.append((rows_p, fk.reshape(B, T, 2, 64), fv.reshape(B, T, 2, 64), lf_p, S_p,
                       seq(z_r)[:, -1]))

        z_r, c_q, c_kv, kr1, kr2, fq, fk, fv, fvs, ff, gates = _in_proj(xs, g_mix, dw["w_in"])
        r, lw, k2, v, a_s, b_s, g = _rwkv_prep_tok(z_r, state_rwkv_shift[l], rparams)
        hd = lambda a: a.reshape(N, RWKV_HEADS, 1, 64)
        y, S_s = _wkv_step(state_rwkv[l], hd(r), hd(lw), hd(k2), hd(v), hd(a_s), hd(b_s))
        y_r = _rwkv_post(y.reshape(N, 512), r, k2, v, g, rw["lnw"], rw["lnb"], rw["rk"],
                         rw["ones"])
        one = lambda a: a.reshape(1, N, a.shape[-1])
        q_m, rows_s, _ = _mla_prep(one(c_q), one(c_kv), one(kr1), one(kr2), cos_s, sin_s, mw)
        q_m = jnp.swapaxes(q_m[0], 0, 1)
        q_rope = q_m[:, :, 256:].reshape(N, MLA_HEADS, 2, MLA_HEADS, 16)
        q_rope = jnp.stack([q_rope[:, h, :, h, :] for h in range(MLA_HEADS)], axis=1)
        q288 = jnp.concatenate([q_m[:, :, :256], q_rope.reshape(N, MLA_HEADS, 32)], axis=-1)
        o_m = _mla_decode(page_table, q288, rows_s.reshape(N, 1, MLA_ROW), cache_mla_t, l)
        y_m = _matmul(o_m.reshape(N, MLA_HEADS * 256), mw["wuv"], BF16)
        three = lambda a: a.reshape(N, 1, a.shape[-1])
        o_f, lf_s = _fox_decode(page_table, fq.astype(BF16).reshape(N, FOX_HEADS, 128), three(fk),
                                three(fv), three(ff), dw["bf"], cache_fk, cache_fv,
                                cache_fox_logf_t, l)
        xs = _merge(xs, y_r, y_m, o_f.reshape(N, 512), gates, dw["wr"], dw["wm"], dw["wf"],
                    dw["wo"])
        xs = _ffn(xs, g_ffn, dw["wu"], dw["wd"])
        outs_s.append((rows_s.reshape(N, 1, MLA_ROW), fk.reshape(N, 1, 2, 64),
                       fv.reshape(N, 1, 2, 64), lf_s, S_s, z_r))

    g_fin = norm_final_g.reshape(1, D)
    y_prompt = _final_norm(xp, g_fin).reshape(B, T, D)
    y_sample = _final_norm(xs, g_fin).reshape(N, 1, D)
    stack = lambda outs: tuple(jnp.stack(t) for t in zip(*outs))
    return (y_prompt, y_sample) + stack(outs_p) + stack(outs_s)
```

```python
import functools

import jax
import jax.numpy as jnp
from jax import lax
from jax.experimental import pallas as pl
from jax.experimental.pallas import tpu as pltpu

F32 = jnp.float32
BF16 = jnp.bfloat16

D_MODEL = 1024
DEPTH = 4
PAGE_SIZE = 128
RWKV_HEADS = 8
RWKV_HEAD_DIM = 64
RWKV_DIM = 512
RWKV_COLS = 1792
GN_EPS = 64e-5
WKV_CHUNK = 64
WKV_GROUP = 4
MLA_HEADS = 8
MLA_Q_LORA = 384
MLA_KV_LORA = 256
MLA_NOPE = 64
MLA_ROPE = 32
MLA_V_DIM = 64
MLA_DIM = 512
MLA_ROW = 288
MLA_SCALE = (MLA_NOPE + MLA_ROPE) ** -0.5
ROPE_THETA = 10000.0
FOX_HEADS = 8
FOX_KV_HEADS = 2
FOX_GROUP = 4
FOX_HEAD_DIM = 64
FOX_DIM = 512
FOX_SCALE = FOX_HEAD_DIM ** -0.5
D_FF = 4096
RMS_EPS = 1e-6
NEG_INF = -1e30

IN_SEGS = (
    ("z_r", 1792), ("c_q", 384), ("c_kv", 256), ("kr1", 128), ("kr2", 128),
    ("fq", 1024), ("fk", 128), ("fv", 128), ("fvs", 128), ("ff", 128), ("gates", 3072),
)
IN_COLS_P = sum(w for _, w in IN_SEGS)

VMEM_LIMIT_BYTES = 56 * 1024 * 1024


def _cparams(sem):
    return pltpu.CompilerParams(dimension_semantics=sem, vmem_limit_bytes=VMEM_LIMIT_BYTES)


def _const_spec(shape):
    n = len(shape)
    return pl.BlockSpec(shape, lambda *_: (0,) * n)


def _dot(a, b):
    return jnp.dot(a, b, preferred_element_type=F32)


def _dot_nt(a, b):
    return lax.dot_general(a, b, (((1,), (1,)), ((), ())), preferred_element_type=F32)


def _dot_tn(a, b):
    return lax.dot_general(a, b, (((0,), (0,)), ((), ())), preferred_element_type=F32)


def _split_dot(x, m):
    hi = x.astype(BF16)
    lo = (x - hi.astype(F32)).astype(BF16)
    return _dot(hi, m) + _dot(lo, m)


def _bf16_terms(x, n=3):
    terms = []
    for _ in range(n):
        t = x.astype(BF16)
        terms.append(t)
        x = x - t.astype(F32)
    return terms


def _sum_dot_rhs(m, x):
    out = None
    for t in _bf16_terms(x):
        out = _dot(m, t) if out is None else out + _dot(m, t)
    return out


def _sum_dot_lhs(x, m):
    out = None
    for t in _bf16_terms(x):
        out = _dot(t, m) if out is None else out + _dot(t, m)
    return out


def _rms(x, g):
    return x * lax.rsqrt(jnp.mean(x * x, axis=-1, keepdims=True) + RMS_EPS) * g


def _sigmoid(x):
    return 1.0 / (1.0 + jnp.exp(-x))


def _log_sigmoid(x):
    return -(jnp.maximum(-x, 0.0) + jnp.log(1.0 + jnp.exp(-jnp.abs(x))))


def _softplus(x):
    return jnp.maximum(x, 0.0) + jnp.log(1.0 + jnp.exp(-jnp.abs(x)))


def _in_proj_kernel(x_ref, g_ref, w_ref, *out_refs):
    ub = _rms(x_ref[...], g_ref[...]).astype(BF16)
    off = 0
    for o_ref, (_, width) in zip(out_refs, IN_SEGS):
        o_ref[...] = _dot(ub, w_ref[:, off:off + width]).astype(o_ref.dtype)
        off += width


def _in_proj(x, g, w):
    m = x.shape[0]
    tm = min(256, m)
    outs = [jax.ShapeDtypeStruct((m, wd), F32) for _, wd in IN_SEGS]
    return pl.pallas_call(
        _in_proj_kernel,
        grid=(m // tm,),
        in_specs=[pl.BlockSpec((tm, D_MODEL), lambda i: (i, 0)),
                  _const_spec((1, D_MODEL)),
                  _const_spec((D_MODEL, IN_COLS_P))],
        out_specs=[pl.BlockSpec((tm, wd), lambda i: (i, 0)) for _, wd in IN_SEGS],
        out_shape=outs,
        compiler_params=_cparams(("parallel",)),
        name="in_proj",
    )(x, g, w)


def _rwkv_prep_math(z, zprev, mu, w0, w2p, a0, a2p, g2, kk_w, ka_w, ones_bd):
    zs = z + (zprev - z) * mu
    r = zs[:, 0:512]
    k = zs[:, 512:1024]
    v = zs[:, 1024:1536]
    zwa = zs[:, 1536:1664]
    zg = zs[:, 1664:1792]
    w = -_softplus(-(w0 + _dot(jnp.tanh(zwa).astype(BF16), w2p))) - 0.5
    a = _sigmoid(a0 + _dot(zwa.astype(BF16), a2p))
    g = _dot(_sigmoid(zg).astype(BF16), g2)
    kk = k * kk_w
    ss = _split_dot(kk * kk, ones_bd)
    kkn = kk * lax.rsqrt(jnp.maximum(ss, 1e-24))
    k2 = k * (1.0 + (a - 1.0) * ka_w)
    lw = -jnp.exp(w)
    return r, lw, k2, v, -kkn, kkn * a, g


def _rwkv_prep_seq_kernel(z_ref, prev_ref, mu_ref, w0_ref, w2_ref, a0_ref, a2_ref, g2_ref,
                          kk_ref, ka_ref, ones_ref,
                          r_ref, lw_ref, k_ref, v_ref, a_ref, b_ref, g_ref, carry):
    t = pl.program_id(1)

    @pl.when(t == 0)
    def _():
        carry[...] = jnp.broadcast_to(prev_ref[0], carry.shape)

    z = z_ref[0]
    rolled = pltpu.roll(z, 1, axis=0)
    row = lax.broadcasted_iota(jnp.int32, z.shape, 0)
    zprev = jnp.where(row == 0, carry[0:1, :], rolled)
    carry[...] = jnp.broadcast_to(z[z.shape[0] - 1:, :], carry.shape)
    outs = _rwkv_prep_math(z, zprev, mu_ref[...], w0_ref[...], w2_ref[...], a0_ref[...],
                           a2_ref[...], g2_ref[...], kk_ref[...], ka_ref[...], ones_ref[...])
    for o_ref, val in zip((r_ref, lw_ref, k_ref, v_ref, a_ref, b_ref, g_ref), outs):
        o_ref[0] = val


def _rwkv_prep_tok_kernel(z_ref, zprev_ref, mu_ref, w0_ref, w2_ref, a0_ref, a2_ref, g2_ref,
                          kk_ref, ka_ref, ones_ref,
                          r_ref, lw_ref, k_ref, v_ref, a_ref, b_ref, g_ref):
    outs = _rwkv_prep_math(z_ref[...], zprev_ref[...], mu_ref[...], w0_ref[...], w2_ref[...],
                           a0_ref[...], a2_ref[...], g2_ref[...], kk_ref[...], ka_ref[...],
                           ones_ref[...])
    for o_ref, val in zip((r_ref, lw_ref, k_ref, v_ref, a_ref, b_ref, g_ref), outs):
        o_ref[...] = val


def _rwkv_param_specs():
    return [_const_spec((1, RWKV_COLS)), _const_spec((1, 512)), _const_spec((128, 512)),
            _const_spec((1, 512)), _const_spec((128, 512)), _const_spec((128, 512)),
            _const_spec((1, 512)), _const_spec((1, 512)), _const_spec((512, 512))]


def _rwkv_prep_seq(z, prev, params):
    b, t, _ = z.shape
    tt = min(256, t)
    outs = [jax.ShapeDtypeStruct((b, t, 512), F32)] * 7
    return pl.pallas_call(
        _rwkv_prep_seq_kernel,
        grid=(b, t // tt),
        in_specs=[pl.BlockSpec((1, tt, RWKV_COLS), lambda i, j: (i, j, 0)),
                  pl.BlockSpec((1, 1, RWKV_COLS), lambda i, j: (i, 0, 0))] + _rwkv_param_specs(),
        out_specs=[pl.BlockSpec((1, tt, 512), lambda i, j: (i, j, 0))] * 7,
        out_shape=outs,
        scratch_shapes=[pltpu.VMEM((8, RWKV_COLS), F32)],
        compiler_params=_cparams(("parallel", "arbitrary")),
        name="rwkv_prep_seq",
    )(z, prev, *params)


def _rwkv_prep_tok(z, zprev, params):
    n = z.shape[0]
    outs = [jax.ShapeDtypeStruct((n, 512), F32)] * 7
    return pl.pallas_call(
        _rwkv_prep_tok_kernel,
        grid=(1,),
        in_specs=[_const_spec((n, RWKV_COLS)), _const_spec((n, RWKV_COLS))] + _rwkv_param_specs(),
        out_specs=[_const_spec((n, 512))] * 7,
        out_shape=outs,
        compiler_params=_cparams(("arbitrary",)),
        name="rwkv_prep_tok",
    )(z, zprev, *params)


def _block_diag(y, bd_mask):
    yb = y.astype(BF16)
    return jnp.where(bd_mask, jnp.concatenate([yb] * WKV_GROUP, axis=0), jnp.zeros((), BF16))


def _wkv_seq_kernel(r_ref, lw_ref, k_ref, v_ref, a_ref, b_ref, y_ref, s_ref, st):
    c = pl.program_id(1)
    C = WKV_CHUNK
    W = WKV_GROUP * RWKV_HEAD_DIM

    @pl.when(c == 0)
    def _():
        st[...] = jnp.zeros_like(st)

    ri = lax.broadcasted_iota(jnp.int32, (W, W), 0)
    ci = lax.broadcasted_iota(jnp.int32, (W, W), 1)
    bd_mask = (ri // RWKV_HEAD_DIM) == (ci // RWKV_HEAD_DIM)
    tr = lax.broadcasted_iota(jnp.int32, (C, W), 0)
    tc = lax.broadcasted_iota(jnp.int32, (C, W), 1) % C
    strict = tr > tc
    incl = tr >= tc
    eye = (tr == tc).astype(F32)
    cr = lax.broadcasted_iota(jnp.int32, (C, C), 0)
    cc = lax.broadcasted_iota(jnp.int32, (C, C), 1)
    tril = (cr >= cc).astype(BF16)

    chains = [(n, slice(g * W, (g + 1) * W), g) for n in range(r_ref.shape[0])
              for g in range(RWKV_HEADS // WKV_GROUP)]
    each = lambda f, *cols: [f(*xs) for xs in zip(*cols)]
    bd = lambda y: _block_diag(y, bd_mask)
    b16 = lambda y: y.astype(BF16)
    load = lambda ref: [ref[n, :, sl] for n, sl, _ in chains]
    r, lw, k, v, a, b = (load(ref) for ref in (r_ref, lw_ref, k_ref, v_ref, a_ref, b_ref))
    hT = [st[n, g] for n, _, g in chains]
    L = each(lambda x: _sum_dot_rhs(tril, x), lw)
    e_in = each(jnp.exp, L)
    e_neg = each(lambda x: jnp.exp(-x), L)
    at = each(lambda x, y, z: x * jnp.exp(y - z), a, L, lw)
    rt = each(jnp.multiply, r, e_in)
    bt = each(jnp.multiply, b, e_neg)
    kt = each(jnp.multiply, k, e_neg)
    lhs = each(lambda x, y: b16(jnp.concatenate([x, y], axis=0)), at, rt)
    gb = each(lambda x, y: _dot_nt(x, bd(y)), lhs, bt)
    gk = each(lambda x, y: _dot_nt(x, bd(y)), lhs, kt)
    a_ab = each(lambda x: jnp.where(strict, x[:C], 0.0), gb)
    a_ak = each(lambda x: jnp.where(strict, x[:C], 0.0), gk)
    a_rb = each(lambda x: jnp.where(incl, x[C:], 0.0), gb)
    a_rk = each(lambda x: jnp.where(incl, x[C:], 0.0), gk)
    inv = each(lambda x: eye + x, a_ab)
    ap = a_ab
    span = 1
    while span * 2 < C:
        ap = each(lambda x: _dot(b16(x), bd(x)), ap)
        inv = each(lambda x, y: x + _dot(b16(x), bd(y)), inv, ap)
        span *= 2
    hTb = each(b16, hT)
    v_bd = each(bd, v)
    rhs = each(lambda x, h, y, z: _dot_nt(b16(x), h) + _dot(b16(y), z), at, hTb, a_ak, v_bd)
    u = each(lambda x, y: _dot(b16(x), bd(y)), inv, rhs)
    y = each(lambda x, h, p, q, s, z: _dot_nt(b16(x), h) + _dot(b16(p), bd(q)) + _dot(b16(s), z),
             rt, hTb, a_rb, u, a_rk, v_bd)
    upd = each(lambda p, q, s, z: _dot_tn(b16(p), b16(q)) + _dot_tn(b16(s), b16(z)), u, bt, v, kt)
    for (n, sl, g), yi, hi, ui, ei in zip(chains, y, hT, upd, e_in):
        y_ref[n, :, sl] = yi
        st[n, g] = jnp.where(bd_mask, (hi + ui) * ei[C - 1:C, :], 0.0)

    @pl.when(c == pl.num_programs(1) - 1)
    def _():
        for n in range(r_ref.shape[0]):
            for h in range(RWKV_HEADS):
                g, j = divmod(h, WKV_GROUP)
                lo = j * RWKV_HEAD_DIM
                s_ref[n, h] = st[n, g, lo:lo + RWKV_HEAD_DIM, lo:lo + RWKV_HEAD_DIM]


WKV_SEQS = 4


def _wkv_seq(r, lw, k, v, a, b):
    bsz, t, _ = r.shape
    nb = WKV_SEQS
    spec = pl.BlockSpec((nb, WKV_CHUNK, 512), lambda i, j: (i, j, 0))
    w = WKV_GROUP * RWKV_HEAD_DIM
    return pl.pallas_call(
        _wkv_seq_kernel,
        grid=(bsz // nb, t // WKV_CHUNK),
        in_specs=[spec] * 6,
        out_specs=[spec, pl.BlockSpec((nb, RWKV_HEADS, 64, 64), lambda i, j: (i, 0, 0, 0))],
        out_shape=[jax.ShapeDtypeStruct((bsz, t, 512), F32),
                   jax.ShapeDtypeStruct((bsz, RWKV_HEADS, 64, 64), F32)],
        scratch_shapes=[pltpu.VMEM((nb, RWKV_HEADS // WKV_GROUP, w, w), F32)],
        compiler_params=_cparams(("parallel", "arbitrary")),
        name="wkv_seq",
    )(r, lw, k, v, a, b)


def _wkv_step_kernel(s_ref, r_ref, lw_ref, k_ref, v_ref, a_ref, b_ref, y_ref, so_ref):
    s = s_ref[...]
    eye = (lax.broadcasted_iota(jnp.int32, (64, 64), 0)
           == lax.broadcasted_iota(jnp.int32, (64, 64), 1))[None, None]
    row = lambda ref: ref[...]
    w = jnp.exp(row(lw_ref))
    vcol = jnp.sum(jnp.where(eye, row(v_ref), 0.0), axis=-1, keepdims=True)
    sa = jnp.sum(s * row(a_ref), axis=-1, keepdims=True)
    s = s * w + sa * row(b_ref) + vcol * row(k_ref)
    ycol = jnp.sum(s * row(r_ref), axis=-1, keepdims=True)
    y_ref[...] = jnp.sum(jnp.where(eye, ycol, 0.0), axis=-2, keepdims=True)
    so_ref[...] = s


def _wkv_step(s, r, lw, k, v, a, b):
    n = s.shape[0]
    nb = 8
    vspec = pl.BlockSpec((nb, RWKV_HEADS, 1, 64), lambda i: (i, 0, 0, 0))
    sspec = pl.BlockSpec((nb, RWKV_HEADS, 64, 64), lambda i: (i, 0, 0, 0))
    return pl.pallas_call(
        _wkv_step_kernel,
        grid=(n // nb,),
        in_specs=[sspec] + [vspec] * 6,
        out_specs=[vspec, sspec],
        out_shape=[jax.ShapeDtypeStruct((n, RWKV_HEADS, 1, 64), F32),
                   jax.ShapeDtypeStruct(s.shape, F32)],
        compiler_params=_cparams(("parallel",)),
        name="wkv_step",
    )(s, r, lw, k, v, a, b)


def _rwkv_post_kernel(y_ref, r_ref, k_ref, v_ref, g_ref, lnw_ref, lnb_ref, rk_ref, ones_ref,
                      o_ref):
    y = y_ref[...]
    ones = ones_ref[...]
    inv_n = 1.0 / RWKV_HEAD_DIM
    mu = _split_dot(y, ones) * inv_n
    d = y - mu
    var = _split_dot(d * d, ones) * inv_n
    yn = d * lax.rsqrt(var + GN_EPS) * lnw_ref[...] + lnb_ref[...]
    bonus = _split_dot(r_ref[...] * k_ref[...] * rk_ref[...], ones) * v_ref[...]
    o_ref[...] = ((yn + bonus) * g_ref[...]).astype(o_ref.dtype)


def _rwkv_post(y, r, k, v, g, lnw, lnb, rk, ones_bd):
    m = y.shape[0]
    tm = min(512, m)
    spec = pl.BlockSpec((tm, 512), lambda i: (i, 0))
    return pl.pallas_call(
        _rwkv_post_kernel,
        grid=(m // tm,),
        in_specs=[spec] * 5 + [_const_spec((1, 512))] * 3 + [_const_spec((512, 512))],
        out_specs=spec,
        out_shape=jax.ShapeDtypeStruct((m, 512), BF16),
        compiler_params=_cparams(("parallel",)),
        name="rwkv_post",
    )(y, r, k, v, g, lnw, lnb, rk, ones_bd)


def _rope_lane_mask(h, shape):
    lane = lax.broadcasted_iota(jnp.int32, shape, len(shape) - 1) % 128
    return (lane >= 16 * h) & (lane < 16 * (h + 1))


def _mla_prep_kernel(zq_ref, zc_ref, zk1_ref, zk2_ref, cos_ref, sin_ref, gq_ref, wn_ref, wr1_ref,
                     wr2_ref, gkv_ref, wuk_ref, q_ref, rows_ref, k_ref):
    cos, sin = cos_ref[...], sin_ref[...]
    cq = _rms(zq_ref[0], gq_ref[...]).astype(BF16)
    qn = _dot(cq, wn_ref[...]) * MLA_SCALE
    q1 = _dot(cq, wr1_ref[...]) * MLA_SCALE
    q2 = _dot(cq, wr2_ref[...]) * MLA_SCALE
    qr = jnp.concatenate([q1 * cos - q2 * sin, q1 * sin + q2 * cos], axis=-1).astype(BF16)
    for p in range(MLA_HEADS // 2):
        ql = _dot(qn[:, 128 * p:128 * (p + 1)].astype(BF16), wuk_ref[p]).astype(BF16)
        for j in range(2):
            h = 2 * p + j
            q_ref[0, h, :, 0:256] = ql[:, 256 * j:256 * (j + 1)]
            q_ref[0, h, :, 256:512] = jnp.where(_rope_lane_mask(h, qr.shape), qr,
                                                jnp.zeros((), BF16))
    ckv = _rms(zc_ref[0], gkv_ref[...])
    z1, z2 = zk1_ref[0], zk2_ref[0]
    k1 = z1 * cos - z2 * sin
    k2 = z1 * sin + z2 * cos
    rows_ref[0, :, 0:256] = ckv
    rows_ref[0, :, 256:272] = k1[:, 0:16]
    rows_ref[0, :, 272:288] = k2[:, 0:16]
    k_ref[0] = jnp.concatenate([ckv, k1, k2], axis=-1).astype(BF16)


def _mla_prep(zq, zc, zk1, zk2, cos8, sin8, p):
    b, t, _ = zq.shape
    tt = min(256, t)
    io = lambda wd: pl.BlockSpec((1, tt, wd), lambda i, j: (i, j, 0))
    tab = pl.BlockSpec((tt, 128), lambda i, j: (j, 0))
    return pl.pallas_call(
        _mla_prep_kernel,
        grid=(b, t // tt),
        in_specs=[io(384), io(256), io(128), io(128), tab, tab,
                  _const_spec((1, 384)), _const_spec((384, 512)), _const_spec((384, 128)),
                  _const_spec((384, 128)), _const_spec((1, 256)), _const_spec((4, 128, 512))],
        out_specs=[pl.BlockSpec((1, MLA_HEADS, tt, 512), lambda i, j: (i, 0, j, 0)),
                   io(MLA_ROW), io(512)],
        out_shape=[jax.ShapeDtypeStruct((b, MLA_HEADS, t, 512), BF16),
                   jax.ShapeDtypeStruct((b, t, MLA_ROW), F32),
                   jax.ShapeDtypeStruct((b, t, 512), BF16)],
        compiler_params=_cparams(("parallel", "parallel")),
        name="mla_prep",
    )(zq, zc, zk1, zk2, cos8, sin8, p["gq"], p["wq_n"], p["wq_r1"], p["wq_r2"], p["gkv"],
      p["wuk"])


def _flash_init(m_scr, l_scr, acc_scr):
    m_scr[...] = jnp.full_like(m_scr, NEG_INF)
    l_scr[...] = jnp.zeros_like(l_scr)
    acc_scr[...] = jnp.zeros_like(acc_scr)


def _flash_step(q, k, v, diagonal, tq, m_scr, l_scr, acc_scr):
    tk = k.shape[0]
    rows = q.shape[0]
    s = _dot_nt(q, k)
    if diagonal:
        qpos = lax.broadcasted_iota(jnp.int32, (rows, tk), 0) % tq
        kpos = lax.broadcasted_iota(jnp.int32, (rows, tk), 1)
        s = jnp.where(kpos <= qpos, s, NEG_INF)
    m_prev = m_scr[...]
    m_new = jnp.maximum(m_prev, jnp.max(s, axis=-1, keepdims=True))
    alpha = jnp.exp(m_prev - m_new)
    p = jnp.exp(s - jnp.concatenate([m_new] * (tk // 128), axis=1))
    psum = p[:, 0:128]
    for j in range(1, tk // 128):
        psum = psum + p[:, 128 * j:128 * (j + 1)]
    l_scr[...] = alpha * l_scr[...] + psum
    acc_scr[...] = (jnp.concatenate([alpha] * (v.shape[1] // 128), axis=1) * acc_scr[...]
                    + _dot(p.astype(BF16), v))
    m_scr[...] = m_new


def _flash_result(l_scr, acc_scr):
    return acc_scr[...] / jnp.sum(l_scr[...], axis=-1, keepdims=True)


def _causal_tiles(n):
    pairs = [(q, k) for q in range(n) for k in range(q + 1)]
    return (jnp.array([p[0] for p in pairs], jnp.int32),
            jnp.array([p[1] for p in pairs], jnp.int32))


def _mla_attn_kernel(qt_ref, kt_ref, q_ref, k_ref, wuv_ref, o_ref, m_scr, l_scr, acc_scr):
    step = pl.program_id(1)
    qi, kj = qt_ref[step], kt_ref[step]
    tq = q_ref.shape[2]

    @pl.when(kj == 0)
    def _():
        _flash_init(m_scr, l_scr, acc_scr)

    def step(diagonal):
        k = k_ref[0]
        _flash_step(q_ref[0].reshape(MLA_HEADS * tq, 512), k, k[:, 0:MLA_KV_LORA], diagonal, tq,
                    m_scr, l_scr, acc_scr)

    pl.when(kj < qi)(functools.partial(step, False))
    pl.when(kj == qi)(functools.partial(step, True))

    @pl.when(kj == qi)
    def _():
        o = _flash_result(l_scr, acc_scr).astype(BF16)
        o = jnp.concatenate([o[h * tq:(h + 1) * tq] for h in range(MLA_HEADS)], axis=-1)
        o_ref[0] = _dot(o, wuv_ref[...]).astype(o_ref.dtype)


def _mla_attn(q, k, wuv):
    b, _, t, _ = q.shape
    tq = tk = min(256, t)
    n = t // tq
    rows = MLA_HEADS * tq
    qt, kt = _causal_tiles(n)
    grid_spec = pltpu.PrefetchScalarGridSpec(
        num_scalar_prefetch=2,
        grid=(b, qt.shape[0]),
        in_specs=[pl.BlockSpec((1, MLA_HEADS, tq, 512), lambda i, s, qt, kt: (i, 0, qt[s], 0)),
                  pl.BlockSpec((1, tk, 512), lambda i, s, qt, kt: (i, kt[s], 0)),
                  pl.BlockSpec((2048, 512), lambda i, s, qt, kt: (0, 0))],
        out_specs=pl.BlockSpec((1, tq, 512), lambda i, s, qt, kt: (i, qt[s], 0)),
        scratch_shapes=[pltpu.VMEM((rows, 128), F32), pltpu.VMEM((rows, 128), F32),
                        pltpu.VMEM((rows, MLA_KV_LORA), F32)],
    )
    return pl.pallas_call(
        _mla_attn_kernel,
        grid_spec=grid_spec,
        out_shape=jax.ShapeDtypeStruct((b, t, 512), BF16),
        compiler_params=_cparams(("parallel", "arbitrary")),
        name="mla_attn",
    )(qt, kt, q, k, wuv)


FOX_C_PARTS = 3
FOX_AUX_BASE = 8


def _fox_aux_constants():
    n = FOX_C_PARTS
    pk = jnp.zeros((n * 128, 128), F32)
    pq = jnp.zeros((n * 128, FOX_HEADS * 128), F32)
    kconst = jnp.zeros((1, 128), F32).at[0, 0:n].set(1.0)
    qconst = jnp.zeros((1, FOX_HEADS * 128), F32)
    for h in range(FOX_HEADS):
        for i in range(n):
            pk = pk.at[i * 128 + h, FOX_AUX_BASE + n * h + i].set(1.0)
            pq = pq.at[i * 128 + h, 128 * h + i].set(1.0)
            qconst = qconst.at[0, 128 * h + FOX_AUX_BASE + n * h + i].set(-1.0)
    return pk.astype(BF16), pq.astype(BF16), kconst, qconst


def _fox_prep_kernel(fq_ref, fk_ref, fv_ref, fvs_ref, ff_ref, bf_ref, pk_ref, pq_ref, kc_ref,
                     qc_ref, lf_ref, q_ref, k_ref, v_ref, carry):
    t = pl.program_id(1)
    tt = ff_ref.shape[1]

    @pl.when(t == 0)
    def _():
        carry[...] = jnp.zeros_like(carry)

    lane = lax.broadcasted_iota(jnp.int32, (tt, 128), 1)
    lf = jnp.where(lane < FOX_HEADS, _log_sigmoid(ff_ref[0] + bf_ref[...]), 0.0)
    lf_ref[0] = lf[:, 0:FOX_HEADS]
    ri = lax.broadcasted_iota(jnp.int32, (tt, tt), 0)
    ci = lax.broadcasted_iota(jnp.int32, (tt, tt), 1)
    c = _sum_dot_rhs((ri >= ci).astype(BF16), lf) + carry[0:1, :]
    carry[...] = jnp.broadcast_to(c[tt - 1:tt, :], carry.shape)
    parts = jnp.concatenate(_bf16_terms(c, FOX_C_PARTS), axis=1)
    kaux = _dot(parts, pk_ref[...]) + kc_ref[...]
    qaux = _dot(parts, pq_ref[...]) + qc_ref[...]
    k_ref[0] = jnp.concatenate([fk_ref[0], kaux], axis=1).astype(BF16)
    v_ref[0] = jnp.concatenate([fv_ref[0], fvs_ref[0]], axis=1).astype(BF16)
    for h in range(FOX_HEADS):
        sl = slice(128 * h, 128 * (h + 1))
        q_ref[0, h] = jnp.concatenate([fq_ref[0, :, sl], qaux[:, sl]], axis=1).astype(BF16)


def _fox_prep(fq, fk, fv, fvs, ff, bf, aux):
    b, t, _ = ff.shape
    tt = min(256, t)
    io = lambda wd: pl.BlockSpec((1, tt, wd), lambda i, j: (i, j, 0))
    n = FOX_C_PARTS * 128
    return pl.pallas_call(
        _fox_prep_kernel,
        grid=(b, t // tt),
        in_specs=[io(1024), io(128), io(128), io(128), io(128), _const_spec((1, 128)),
                  _const_spec((n, 128)), _const_spec((n, 1024)), _const_spec((1, 128)),
                  _const_spec((1, 1024))],
        out_specs=[io(FOX_HEADS),
                   pl.BlockSpec((1, FOX_HEADS, tt, 256), lambda i, j: (i, 0, j, 0)),
                   io(256), io(256)],
        out_shape=[jax.ShapeDtypeStruct((b, t, FOX_HEADS), F32),
                   jax.ShapeDtypeStruct((b, FOX_HEADS, t, 256), BF16),
                   jax.ShapeDtypeStruct((b, t, 256), BF16),
                   jax.ShapeDtypeStruct((b, t, 256), BF16)],
        scratch_shapes=[pltpu.VMEM((8, 128), F32)],
        compiler_params=_cparams(("parallel", "arbitrary")),
        name="fox_prep",
    )(fq, fk, fv, fvs, ff, bf, *aux)


def _fox_attn_kernel(qt_ref, kt_ref, q_ref, k_ref, v_ref, o_ref, m_scr, l_scr, acc_scr):
    step = pl.program_id(1)
    qi, kj = qt_ref[step], kt_ref[step]
    tq = q_ref.shape[2]

    @pl.when(kj == 0)
    def _():
        _flash_init(m_scr, l_scr, acc_scr)

    def step(diagonal):
        _flash_step(q_ref[0].reshape(FOX_HEADS * tq, 256), k_ref[0], v_ref[0], diagonal, tq,
                    m_scr, l_scr, acc_scr)

    pl.when(kj < qi)(functools.partial(step, False))
    pl.when(kj == qi)(functools.partial(step, True))

    @pl.when(kj == qi)
    def _():
        o = _flash_result(l_scr, acc_scr)
        lane = lax.broadcasted_iota(jnp.int32, (tq, 128), 1)

        def head(h):
            plain = (h % 2 == 0) == (h // FOX_GROUP == 0)
            return o[h * tq:(h + 1) * tq, 0:128] if plain else o[h * tq:(h + 1) * tq, 128:256]

        for p2 in range(FOX_HEADS // 2):
            o_ref[0, :, 128 * p2:128 * (p2 + 1)] = jnp.where(
                lane < 64, head(2 * p2), head(2 * p2 + 1)).astype(o_ref.dtype)


def _fox_attn(q, k, v):
    b, _, t, _ = q.shape
    tq = tk = min(256, t)
    n = t // tq
    rows = FOX_HEADS * tq
    qt, kt = _causal_tiles(n)
    kspec = pl.BlockSpec((1, tk, 256), lambda i, s, qt, kt: (i, kt[s], 0))
    grid_spec = pltpu.PrefetchScalarGridSpec(
        num_scalar_prefetch=2,
        grid=(b, qt.shape[0]),
        in_specs=[pl.BlockSpec((1, FOX_HEADS, tq, 256), lambda i, s, qt, kt: (i, 0, qt[s], 0)),
                  kspec, kspec],
        out_specs=pl.BlockSpec((1, tq, 512), lambda i, s, qt, kt: (i, qt[s], 0)),
        scratch_shapes=[pltpu.VMEM((rows, 128), F32), pltpu.VMEM((rows, 128), F32),
                        pltpu.VMEM((rows, 256), F32)],
    )
    return pl.pallas_call(
        _fox_attn_kernel,
        grid_spec=grid_spec,
        out_shape=jax.ShapeDtypeStruct((b, t, 512), BF16),
        compiler_params=_cparams(("parallel", "arbitrary")),
        name="fox_attn",
    )(qt, kt, q, k, v)


DEC_PAGES = 32
DEC_BLOCK = 256


def _page_copies(pt_ref, seq, chunk, slot, layer, pairs, sem):
    out = []
    for j in range(DEC_PAGES):
        page = pt_ref[seq, chunk * DEC_PAGES + j]
        for cache_ref, buf_ref in pairs:
            out.append(pltpu.make_async_copy(cache_ref.at[layer, page], buf_ref.at[slot, j],
                                             sem.at[slot]))
    return out


def _paged_pipeline(pt_ref, pairs, sem, layer):
    n_chunks = pl.num_programs(1)
    step = pl.program_id(0) * n_chunks + pl.program_id(1)
    total = pl.num_programs(0) * n_chunks
    slot = step % 2

    @pl.when(step == 0)
    def _():
        for cp in _page_copies(pt_ref, 0, 0, 0, layer, pairs, sem):
            cp.start()

    nxt = step + 1

    @pl.when(nxt < total)
    def _():
        for cp in _page_copies(pt_ref, nxt // n_chunks, nxt % n_chunks, 1 - slot, layer, pairs, sem):
            cp.start()

    for cp in _page_copies(pt_ref, pl.program_id(0), pl.program_id(1), slot, layer, pairs, sem):
        cp.wait()
    return slot


def _mla_dec_main(slot, q_ref, buf, m_scr, l_scr, acc_scr):
    q = q_ref[0]
    nblk = DEC_PAGES * PAGE_SIZE // DEC_BLOCK
    ppb = DEC_BLOCK // PAGE_SIZE
    kts = [jnp.concatenate([buf[slot, blk * ppb + j] for j in range(ppb)], axis=1).astype(BF16)
           for blk in range(nblk)]
    s = jnp.stack([_dot(q, kt) for kt in kts], axis=0)
    m = m_scr[...]
    m_new = jnp.maximum(m, jnp.max(jnp.max(s, axis=-1, keepdims=True), axis=0))
    alpha = jnp.exp(m - m_new)
    p = jnp.exp(s - m_new[None])
    l_scr[...] = alpha * l_scr[...] + jnp.sum(jnp.sum(p, axis=-1, keepdims=True), axis=0)
    acc = alpha * acc_scr[...]
    for blk in range(nblk):
        acc = acc + _dot_nt(p[blk].astype(BF16), kts[blk][0:MLA_KV_LORA, :])
    acc_scr[...] = acc
    m_scr[...] = m_new


def _mla_dec_final(q_ref, new_ref, o_ref, m_scr, l_scr, acc_scr):
    q = q_ref[0]
    m = m_scr[...]
    new = new_ref[0].astype(BF16).astype(F32)
    s_new = jnp.sum(q.astype(F32) * new, axis=-1, keepdims=True)
    m_fin = jnp.maximum(m, s_new)
    w_old = jnp.exp(m - m_fin)
    p_new = jnp.exp(s_new - m_fin).astype(BF16).astype(F32)
    den = l_scr[...] * w_old + p_new
    o_ref[0] = ((acc_scr[...] * w_old + p_new * new[:, 0:MLA_KV_LORA]) / den).astype(o_ref.dtype)


def _fox_dec_main(slot, q_ref, kbuf, vbuf, lbuf, m_scr, l_scr, acc_scr, run_scr):
    H = FOX_HEADS
    nblk = DEC_PAGES * PAGE_SIZE // DEC_BLOCK
    ppb = DEC_BLOCK // PAGE_SIZE
    R = nblk * H
    q = q_ref[0]
    lf = jnp.concatenate(
        [jnp.concatenate([lbuf[slot, blk * ppb + j] for j in range(ppb)], axis=1)
         for blk in range(nblk)], axis=0)
    ri = lax.broadcasted_iota(jnp.int32, (DEC_BLOCK, DEC_BLOCK), 0)
    ci = lax.broadcasted_iota(jnp.int32, (DEC_BLOCK, DEC_BLOCK), 1)
    cum = _sum_dot_lhs(lf, (ri <= ci).astype(BF16))
    rr = lax.broadcasted_iota(jnp.int32, (2 * R, R), 0)
    rc = lax.broadcasted_iota(jnp.int32, (2 * R, R), 1)
    same_head = (rr % H) == (rc % H)
    tot = cum[:, DEC_BLOCK - 1:DEC_BLOCK]
    tot_b = jnp.broadcast_to(tot, (R, 128))
    sums = _sum_dot_rhs((same_head & ((rr >= R) | (rc // H < rr // H))).astype(BF16), tot_b)
    before, whole = sums[0:R], sums[R:2 * R]
    run = run_scr[...]
    cpast = cum + before[:, 0:1] + run[:, 0:1]
    run_scr[...] = run + whole

    s = jnp.concatenate(
        [_dot(q, jnp.concatenate([kbuf[slot, blk * ppb + j] for j in range(ppb)],
                                 axis=1).astype(BF16)) for blk in range(nblk)], axis=0)
    x = (s - cpast).reshape(nblk, H, DEC_BLOCK)
    m = m_scr[...]
    m_new = jnp.maximum(m, jnp.max(jnp.max(x, axis=-1, keepdims=True), axis=0))
    alpha = jnp.exp(m - m_new)
    p = jnp.exp(x - m_new[None])
    l_scr[...] = alpha * l_scr[...] + jnp.sum(jnp.sum(p, axis=-1, keepdims=True), axis=0)
    acc = alpha * acc_scr[...]
    for blk in range(nblk):
        vt = jnp.concatenate([vbuf[slot, blk * ppb + j] for j in range(ppb)], axis=1).astype(BF16)
        acc = acc + _dot_nt(p[blk].astype(BF16), vt)
    acc_scr[...] = acc
    m_scr[...] = m_new


def _fox_dec_final(q_ref, kn_ref, vn_ref, ffn_ref, bf_ref, o_ref, lf_ref, m_scr, l_scr, acc_scr,
                   run_scr):
    H = FOX_HEADS
    q = q_ref[0]
    diag = (lax.broadcasted_iota(jnp.int32, (H, 128), 0)
            == lax.broadcasted_iota(jnp.int32, (H, 128), 1))
    lf_new_row = _log_sigmoid(ffn_ref[0] + bf_ref[...])
    lf_ref[0] = lf_new_row[:, 0:H]
    lf_new = jnp.sum(jnp.where(diag, lf_new_row, 0.0), axis=-1, keepdims=True)
    c_new = run_scr[0:H, 0:1] + lf_new
    kn = kn_ref[0].astype(BF16).astype(F32)
    vn = vn_ref[0].astype(BF16).astype(F32)
    s_new = jnp.sum(q.astype(F32) * kn, axis=-1, keepdims=True) + (c_new - c_new)
    m_past = m_scr[...] + c_new
    m_fin = jnp.maximum(m_past, s_new)
    w_old = jnp.exp(m_past - m_fin)
    p_new = jnp.exp(s_new - m_fin).astype(BF16).astype(F32)
    den = l_scr[...] * w_old + p_new
    o = (acc_scr[...] * w_old + p_new * vn) / den
    head = lax.broadcasted_iota(jnp.int32, (H, 64), 0)
    o_ref[0] = jnp.where(head < FOX_GROUP, o[:, 0:64], o[:, 64:128]).astype(o_ref.dtype)


def _decode_kernel(pt_ref, qm_ref, newm_ref, qf_ref, kn_ref, vn_ref, ffn_ref, bf_ref,
                   cm_ref, ck_ref, cv_ref, cl_ref, om_ref, of_ref, lf_ref,
                   mbuf, kbuf, vbuf, lbuf, sem, mm, ml, macc, fm, fl, facc, run_scr, *, layer):
    c = pl.program_id(1)
    slot = _paged_pipeline(pt_ref, [(cm_ref, mbuf), (ck_ref, kbuf), (cv_ref, vbuf), (cl_ref, lbuf)],
                           sem, layer)

    @pl.when(c == 0)
    def _():
        for m_scr, l_scr, acc_scr in ((mm, ml, macc), (fm, fl, facc)):
            m_scr[...] = jnp.full_like(m_scr, NEG_INF)
            l_scr[...] = jnp.zeros_like(l_scr)
            acc_scr[...] = jnp.zeros_like(acc_scr)
        run_scr[...] = jnp.zeros_like(run_scr)

    _mla_dec_main(slot, qm_ref, mbuf, mm, ml, macc)
    _fox_dec_main(slot, qf_ref, kbuf, vbuf, lbuf, fm, fl, facc, run_scr)

    @pl.when(c == pl.num_programs(1) - 1)
    def _():
        _mla_dec_final(qm_ref, newm_ref, om_ref, mm, ml, macc)
        _fox_dec_final(qf_ref, kn_ref, vn_ref, ffn_ref, bf_ref, of_ref, lf_ref, fm, fl, facc, run_scr)


def _decode(page_table, q_m, new_rows, q_f, k_new, v_new, ff_new, bf, cache_m, cache_k, cache_v,
            cache_lt, layer):
    n, n_pages = page_table.shape
    per_seq = lambda *blk: pl.BlockSpec((1,) + blk, lambda i, c, pt: (i,) + (0,) * len(blk))
    hbm = pl.BlockSpec(memory_space=pl.ANY)
    rows = DEC_PAGES * PAGE_SIZE // DEC_BLOCK * FOX_HEADS
    grid_spec = pltpu.PrefetchScalarGridSpec(
        num_scalar_prefetch=1,
        grid=(n, n_pages // DEC_PAGES),
        in_specs=[per_seq(MLA_HEADS, MLA_ROW), per_seq(1, MLA_ROW), per_seq(FOX_HEADS, 128),
                  per_seq(1, 128), per_seq(1, 128), per_seq(1, 128),
                  pl.BlockSpec((1, 128), lambda i, c, pt: (0, 0)), hbm, hbm, hbm, hbm],
        out_specs=[per_seq(MLA_HEADS, MLA_KV_LORA), per_seq(FOX_HEADS, 64), per_seq(1, FOX_HEADS)],
        scratch_shapes=[pltpu.VMEM((2, DEC_PAGES, MLA_ROW, PAGE_SIZE), F32),
                        pltpu.VMEM((2, DEC_PAGES, 128, PAGE_SIZE), F32),
                        pltpu.VMEM((2, DEC_PAGES, 128, PAGE_SIZE), F32),
                        pltpu.VMEM((2, DEC_PAGES, FOX_HEADS, PAGE_SIZE), F32),
                        pltpu.SemaphoreType.DMA((2,)),
                        pltpu.VMEM((MLA_HEADS, 1), F32), pltpu.VMEM((MLA_HEADS, 1), F32),
                        pltpu.VMEM((MLA_HEADS, MLA_KV_LORA), F32),
                        pltpu.VMEM((FOX_HEADS, 1), F32), pltpu.VMEM((FOX_HEADS, 1), F32),
                        pltpu.VMEM((FOX_HEADS, 128), F32), pltpu.VMEM((rows, 128), F32)],
    )
    return pl.pallas_call(
        functools.partial(_decode_kernel, layer=layer),
        grid_spec=grid_spec,
        out_shape=[jax.ShapeDtypeStruct((n, MLA_HEADS, MLA_KV_LORA), BF16),
                   jax.ShapeDtypeStruct((n, FOX_HEADS, 64), BF16),
                   jax.ShapeDtypeStruct((n, 1, FOX_HEADS), F32)],
        compiler_params=_cparams(("arbitrary", "arbitrary")),
        name="decode_attn",
    )(page_table, q_m, new_rows, q_f, k_new, v_new, ff_new, bf, cache_m, cache_k, cache_v, cache_lt)


def _matmul_kernel(x_ref, w_ref, o_ref):
    o_ref[...] = _dot(x_ref[...], w_ref[...]).astype(o_ref.dtype)


def _matmul(x, w, out_dtype):
    m, k = x.shape
    n = w.shape[1]
    tm = min(256, m)
    return pl.pallas_call(
        _matmul_kernel,
        grid=(m // tm,),
        in_specs=[pl.BlockSpec((tm, k), lambda i: (i, 0)), _const_spec((k, n))],
        out_specs=pl.BlockSpec((tm, n), lambda i: (i, 0)),
        out_shape=jax.ShapeDtypeStruct((m, n), out_dtype),
        compiler_params=_cparams(("parallel",)),
        name="matmul",
    )(x, w)


def _merge_kernel(x_ref, yr_ref, ym_ref, yf_ref, g_ref, wr_ref, wm_ref, wf_ref, wo_ref, o_ref):
    merged = (_sigmoid(g_ref[:, 0:1024]) * _dot(yr_ref[...], wr_ref[...])
              + _sigmoid(g_ref[:, 1024:2048]) * _dot(ym_ref[...], wm_ref[...])
              + _sigmoid(g_ref[:, 2048:3072]) * _dot(yf_ref[...], wf_ref[...]))
    o_ref[...] = x_ref[...] + _dot(merged.astype(BF16), wo_ref[...])


def _merge(x, yr, ym, yf, gates, wr, wm, wf, wo):
    m = x.shape[0]
    tm = min(256, m)
    row = lambda wd: pl.BlockSpec((tm, wd), lambda i: (i, 0))
    return pl.pallas_call(
        _merge_kernel,
        grid=(m // tm,),
        in_specs=[row(D_MODEL), row(512), row(512), row(512), row(3072),
                  _const_spec((512, D_MODEL)), _const_spec((512, D_MODEL)),
                  _const_spec((512, D_MODEL)), _const_spec((D_MODEL, D_MODEL))],
        out_specs=row(D_MODEL),
        out_shape=jax.ShapeDtypeStruct((m, D_MODEL), F32),
        compiler_params=_cparams(("parallel",)),
        name="merge",
    )(x, yr, ym, yf, gates, wr, wm, wf, wo)


FFN_CHUNK = 1024


def _ffn_kernel(x_ref, g_ref, wu_ref, wd_ref, o_ref):
    x = x_ref[...]
    ub = _rms(x, g_ref[...]).astype(BF16)
    acc = x
    for c in range(D_FF // FFN_CHUNK):
        sl = slice(c * FFN_CHUNK, (c + 1) * FFN_CHUNK)
        h = jnp.maximum(_dot(ub, wu_ref[:, sl]), 0.0)
        acc = acc + _dot((h * h).astype(BF16), wd_ref[sl, :])
    o_ref[...] = acc


def _ffn(x, g, wu, wd):
    m = x.shape[0]
    tm = min(256, m)
    row = pl.BlockSpec((tm, D_MODEL), lambda i: (i, 0))
    return pl.pallas_call(
        _ffn_kernel,
        grid=(m // tm,),
        in_specs=[row, _const_spec((1, D_MODEL)), _const_spec((D_MODEL, D_FF)),
                  _const_spec((D_FF, D_MODEL))],
        out_specs=row,
        out_shape=jax.ShapeDtypeStruct((m, D_MODEL), F32),
        compiler_params=_cparams(("parallel",)),
        name="ffn",
    )(x, g, wu, wd)


def _final_norm_kernel(x_ref, g_ref, o_ref):
    o_ref[...] = _rms(x_ref[...], g_ref[...])


def _final_norm(x, g):
    m = x.shape[0]
    tm = min(512, m)
    row = pl.BlockSpec((tm, D_MODEL), lambda i: (i, 0))
    return pl.pallas_call(
        _final_norm_kernel,
        grid=(m // tm,),
        in_specs=[row, _const_spec((1, D_MODEL))],
        out_specs=row,
        out_shape=jax.ShapeDtypeStruct((m, D_MODEL), F32),
        compiler_params=_cparams(("parallel",)),
        name="final_norm",
    )(x, g)


def _prep_weights(w_in, rwkv_mu, rwkv_w0, rwkv_w2, rwkv_a0, rwkv_a2, rwkv_g2, rwkv_k_k, rwkv_k_a,
                  rwkv_r_k, rwkv_ln_w, rwkv_ln_b, mla_q_norm_g, mla_w_uq, mla_kv_norm_g, mla_w_uk,
                  mla_w_uv, fox_b_f, w_branch_rwkv, w_branch_mla, w_branch_fox, w_out, w_ffn_up,
                  w_ffn_down):
    L = w_in.shape[0]
    o_m = RWKV_COLS
    o_kr = o_m + MLA_Q_LORA + MLA_KV_LORA
    o_f = o_kr + MLA_ROPE
    o_fk = o_f + FOX_DIM
    o_fv = o_fk + 128
    o_ff = o_fv + 128
    o_g = o_ff + FOX_HEADS
    zeros = lambda n: jnp.zeros((L, D_MODEL, n), w_in.dtype)
    half = MLA_ROPE // 2
    fq = w_in[:, :, o_f:o_fk].reshape(L, D_MODEL, FOX_HEADS, 64) * FOX_SCALE
    fq_tiles = []
    for h in range(FOX_HEADS):
        parts = [fq[:, :, h], zeros(64)]
        fq_tiles.append(jnp.concatenate(parts if h // FOX_GROUP == 0 else parts[::-1], axis=-1))
    fv = w_in[:, :, o_fv:o_ff]
    w_in_p = jnp.concatenate(
        [w_in[:, :, 0:o_kr],
         jnp.tile(w_in[:, :, o_kr:o_kr + half], (1, 1, 8)),
         jnp.tile(w_in[:, :, o_kr + half:o_f], (1, 1, 8))]
        + fq_tiles
        + [w_in[:, :, o_fk:o_fv], fv, jnp.concatenate([fv[:, :, 64:], fv[:, :, :64]], axis=-1),
           w_in[:, :, o_ff:o_g], zeros(128 - FOX_HEADS), w_in[:, :, o_g:]], axis=-1).astype(BF16)
    assert w_in_p.shape[-1] == IN_COLS_P

    pad_rows = lambda w, top: jnp.concatenate(
        [jnp.zeros_like(w), w] if top else [w, jnp.zeros_like(w)], axis=1).astype(BF16)
    row = lambda v: v.reshape(L, 1, -1)
    idx = jnp.arange(512)
    ones_bd = ((idx[:, None] // 64) == (idx[None, :] // 64)).astype(BF16)
    rwkv = dict(mu=row(rwkv_mu), w0=row(rwkv_w0), w2=pad_rows(rwkv_w2, False), a0=row(rwkv_a0),
                a2=pad_rows(rwkv_a2, True), g2=rwkv_g2.astype(BF16), kk=row(rwkv_k_k),
                ka=row(rwkv_k_a), rk=row(rwkv_r_k), lnw=row(rwkv_ln_w), lnb=row(rwkv_ln_b),
                ones=ones_bd)

    uq = mla_w_uq.reshape(L, MLA_Q_LORA, MLA_HEADS, MLA_NOPE + MLA_ROPE)
    wq_n = uq[..., :MLA_NOPE].reshape(L, MLA_Q_LORA, 512)
    wq_r1 = uq[..., MLA_NOPE:MLA_NOPE + half].reshape(L, MLA_Q_LORA, 128)
    wq_r2 = uq[..., MLA_NOPE + half:].reshape(L, MLA_Q_LORA, 128)
    uk = jnp.transpose(mla_w_uk, (0, 2, 3, 1))
    z = jnp.zeros_like(uk[:, 0])
    wuk = jnp.stack([jnp.concatenate([jnp.concatenate([uk[:, 2 * p], z], axis=-1),
                                      jnp.concatenate([z, uk[:, 2 * p + 1]], axis=-1)], axis=1)
                     for p in range(MLA_HEADS // 2)], axis=1)
    uv = jnp.transpose(mla_w_uv, (0, 2, 1, 3))
    zv = jnp.zeros_like(uv[:, 0])
    wuv = jnp.concatenate(
        [jnp.concatenate([zv] * h + [uv[:, h]] + [zv] * (MLA_HEADS - 1 - h), axis=-1)
         for h in range(MLA_HEADS)], axis=1)
    mla = dict(gq=row(mla_q_norm_g), wq_n=wq_n.astype(BF16), wq_r1=wq_r1.astype(BF16),
               wq_r2=wq_r2.astype(BF16), gkv=row(mla_kv_norm_g), wuk=wuk.astype(BF16),
               wuv=wuv.astype(BF16))
    bf = jnp.concatenate([fox_b_f, jnp.zeros((L, 128 - FOX_HEADS), fox_b_f.dtype)],
                         axis=-1).reshape(L, 1, 128)
    dense = dict(w_in=w_in_p, wr=w_branch_rwkv.astype(BF16), wm=w_branch_mla.astype(BF16),
                 wf=w_branch_fox.astype(BF16), wo=w_out.astype(BF16), wu=w_ffn_up.astype(BF16),
                 wd=w_ffn_down.astype(BF16), bf=bf)
    return rwkv, mla, dense


def _rope_tables(pos):
    half = MLA_ROPE // 2
    inv = ROPE_THETA ** (-jnp.arange(half, dtype=F32) / half)
    ang = pos.astype(F32)[:, None] * inv[None, :]
    return jnp.tile(jnp.cos(ang), (1, 8)), jnp.tile(jnp.sin(ang), (1, 8))


def kernel(x_prompt, x_sample, cache_mla, cache_fox_k, cache_fox_v, cache_fox_logf, state_rwkv,
           state_rwkv_shift, page_table, norm_mix_g, w_in, rwkv_mu, rwkv_w0, rwkv_w2, rwkv_a0,
           rwkv_a2, rwkv_g2, rwkv_k_k, rwkv_k_a, rwkv_r_k, rwkv_ln_w, rwkv_ln_b, mla_q_norm_g,
           mla_w_uq, mla_kv_norm_g, mla_w_uk, mla_w_uv, fox_b_f, w_branch_rwkv, w_branch_mla,
           w_branch_fox, w_out, norm_ffn_g, w_ffn_up, w_ffn_down, norm_final_g):
    B, T, D = x_prompt.shape
    N = x_sample.shape[0]
    assert x_sample.shape[1] == 1
    n_pages = page_table.shape[1]
    past_len = n_pages * cache_mla.shape[2]
    rwkv_w, mla_w, dense_w = _prep_weights(
        w_in, rwkv_mu, rwkv_w0, rwkv_w2, rwkv_a0, rwkv_a2, rwkv_g2, rwkv_k_k, rwkv_k_a, rwkv_r_k,
        rwkv_ln_w, rwkv_ln_b, mla_q_norm_g, mla_w_uq, mla_kv_norm_g, mla_w_uk, mla_w_uv, fox_b_f,
        w_branch_rwkv, w_branch_mla, w_branch_fox, w_out, w_ffn_up, w_ffn_down)
    cos_p, sin_p = _rope_tables(jnp.arange(T))
    cos_s, sin_s = _rope_tables(jnp.full((N,), past_len))
    cache_mla_t = jnp.swapaxes(cache_mla, 2, 3)
    cache_fox_logf_t = jnp.swapaxes(cache_fox_logf, 2, 3)
    feat_major = lambda c: jnp.transpose(c, (0, 1, 3, 4, 2)).reshape(c.shape[:2] + (128, PAGE_SIZE))
    cache_fk = feat_major(cache_fox_k)
    cache_fv = feat_major(cache_fox_v)
    fox_aux = _fox_aux_constants()

    xp = x_prompt.reshape(B * T, D)
    xs = x_sample.reshape(N, D)
    prev_zero = jnp.zeros((B, 1, RWKV_COLS), F32)
    outs_p, outs_s = [], []
    for l in range(DEPTH):
        rw = {k: (v if k == "ones" else v[l]) for k, v in rwkv_w.items()}
        mw = {k: v[l] for k, v in mla_w.items()}
        dw = {k: v[l] for k, v in dense_w.items()}
        g_mix = norm_mix_g[l].reshape(1, D)
        g_ffn = norm_ffn_g[l].reshape(1, D)
        rparams = (rw["mu"], rw["w0"], rw["w2"], rw["a0"], rw["a2"], rw["g2"], rw["kk"], rw["ka"],
                   rw["ones"])

        z_r, c_q, c_kv, kr1, kr2, fq, fk, fv, fvs, ff, gates = _in_proj(xp, g_mix, dw["w_in"])
        seq = lambda a: a.reshape(B, T, a.shape[-1])
        r, lw, k2, v, a_s, b_s, g = _rwkv_prep_seq(seq(z_r), prev_zero, rparams)
        y, S_p = _wkv_seq(r, lw, k2, v, a_s, b_s)
        flat = lambda a: a.reshape(B * T, a.shape[-1])
        y_r = _rwkv_post(flat(y), flat(r), flat(k2), flat(v), flat(g), rw["lnw"], rw["lnb"],
                         rw["rk"], rw["ones"])
        q_m, rows_p, k_m = _mla_prep(seq(c_q), seq(c_kv), seq(kr1), seq(kr2), cos_p, sin_p, mw)
        y_m = _mla_attn(q_m, k_m, mw["wuv"])
        lf_p, q_f, k_f, v_f = _fox_prep(seq(fq), seq(fk), seq(fv), seq(fvs), seq(ff), dw["bf"],
                                        fox_aux)
        y_f = _fox_attn(q_f, k_f, v_f)
        xp = _merge(xp, y_r, flat(y_m), flat(y_f), gates, dw["wr"], dw["wm"], dw["wf"], dw["wo"])
        xp = _ffn(xp, g_ffn, dw["wu"], dw["wd"])
        outs_p.append((rows_p, fk.reshape(B, T, 2, 64), fv.reshape(B, T, 2, 64), lf_p, S_p,
                       seq(z_r)[:, -1]))

        z_r, c_q, c_kv, kr1, kr2, fq, fk, fv, fvs, ff, gates = _in_proj(xs, g_mix, dw["w_in"])
        r, lw, k2, v, a_s, b_s, g = _rwkv_prep_tok(z_r, state_rwkv_shift[l], rparams)
        hd = lambda a: a.reshape(N, RWKV_HEADS, 1, 64)
        y, S_s = _wkv_step(state_rwkv[l], hd(r), hd(lw), hd(k2), hd(v), hd(a_s), hd(b_s))
        y_r = _rwkv_post(y.reshape(N, 512), r, k2, v, g, rw["lnw"], rw["lnb"], rw["rk"],
                         rw["ones"])
        one = lambda a: a.reshape(1, N, a.shape[-1])
        q_m, rows_s, _ = _mla_prep(one(c_q), one(c_kv), one(kr1), one(kr2), cos_s, sin_s, mw)
        q_m = jnp.swapaxes(q_m[0], 0, 1)
        q_rope = q_m[:, :, 256:].reshape(N, MLA_HEADS, 2, MLA_HEADS, 16)
        q_rope = jnp.stack([q_rope[:, h, :, h, :] for h in range(MLA_HEADS)], axis=1)
        q288 = jnp.concatenate([q_m[:, :, :256], q_rope.reshape(N, MLA_HEADS, 32)], axis=-1)
        three = lambda a: a.reshape(N, 1, a.shape[-1])
        o_m, o_f, lf_s = _decode(page_table, q288, rows_s.reshape(N, 1, MLA_ROW),
                                 fq.astype(BF16).reshape(N, FOX_HEADS, 128), three(fk), three(fv),
                                 three(ff), dw["bf"], cache_mla_t, cache_fk, cache_fv,
                                 cache_fox_logf_t, l)
        y_m = _matmul(o_m.reshape(N, MLA_HEADS * 256), mw["wuv"], BF16)
        xs = _merge(xs, y_r, y_m, o_f.reshape(N, 512), gates, dw["wr"], dw["wm"], dw["wf"],
                    dw["wo"])
        xs = _ffn(xs, g_ffn, dw["wu"], dw["wd"])
        outs_s.append((rows_s.reshape(N, 1, MLA_ROW), fk.reshape(N, 1, 2, 64),
                       fv.reshape(N, 1, 2, 64), lf_s, S_s, z_r))

    g_fin = norm_final_g.reshape(1, D)
    y_prompt = _final_norm(xp, g_fin).reshape(B, T, D)
    y_sample = _final_norm(xs, g_fin).reshape(N, 1, D)
    stack = lambda outs: tuple(jnp.stack(t) for t in zip(*outs))
    return (y_prompt, y_sample) + stack(outs_p) + stack(outs_s)
```
